```python
import jax, jax.numpy as jnp
from jax import lax
import numpy as np

D_MODEL = 2048
BATCH = 8
SEQ = 2048
DEPTH = 2
DEC_BATCH = 16
DEC_SEQ = 32
PAST_LEN = 2048

CHUNK = 64
N_MIXERS = 2
N_LRU_LAYERS = (DEPTH + 1) // 2
N_POOL_LAYERS = DEPTH // 2
D_RNN = D_MODEL
LRU_HEADS = 8
LRU_BLOCK = D_RNN // LRU_HEADS
CONV_WIDTH = 4
LRU_C = 8.0
POOL_WINDOWS = (2, 4, 8, 16)
POOL_GROUPS = 4
POOL_GROUP_DIM = D_MODEL // POOL_GROUPS
POOL_STATE = max(POOL_WINDOWS) - 1
MOE_GROUPS = 4
MOE_EXPERTS_PER_GROUP = 4
N_EXPERTS = MOE_GROUPS * MOE_EXPERTS_PER_GROUP
MOE_TOP_K = 2
D_EXPERT = 512
DEEPNORM_ALPHA = (2.0 * DEPTH) ** 0.25
DEEPNORM_BETA = (8.0 * DEPTH) ** -0.25
LN_EPS = 1e-5

kernel_name = "hawk_pool_hmoe_stream_step"

F32 = jnp.float32


def layer_norm(x, g, b):
    xf = x.astype(F32)
    mu = jnp.mean(xf, axis=-1, keepdims=True)
    var = jnp.mean(jnp.square(xf - mu), axis=-1, keepdims=True)
    y = (xf - mu) * lax.rsqrt(var + LN_EPS) * g.astype(F32) + b.astype(F32)
    return y.astype(x.dtype)


def causal_depthwise_conv(u, prefix, w, b):
    s = u.shape[1]
    ext = jnp.concatenate([prefix.astype(u.dtype), u], axis=1)
    y = b
    for k in range(CONV_WIDTH):
        y = y + ext[:, k:k + s] * w[k]
    return y, ext[:, -(CONV_WIDTH - 1):]


def _lin_combine(left, right):
    a1, b1 = left
    a2, b2 = right
    return a1 * a2, a2 * b1 + b2


def rg_lru_block(x, h0, conv0, w_in, conv_w, conv_b, w_a, b_a, w_x, b_x, lam, w_out):
    bsz, s, _ = x.shape
    proj = jnp.einsum('bsd,de->bse', x, w_in)
    gate_br, u = jnp.split(proj, 2, axis=-1)
    uc, conv_new = causal_depthwise_conv(u, conv0, conv_w, conv_b)
    uh = uc.reshape(bsz, s, LRU_HEADS, LRU_BLOCK)
    r = jax.nn.sigmoid((jnp.einsum('bshi,hij->bshj', uh, w_a) + b_a).astype(F32)).reshape(bsz, s, D_RNN)
    ig = jax.nn.sigmoid((jnp.einsum('bshi,hij->bshj', uh, w_x) + b_x).astype(F32)).reshape(bsz, s, D_RNN)
    log_a = -LRU_C * r * jax.nn.softplus(-lam.astype(F32))
    a = jnp.exp(log_a)
    mult = jnp.sqrt(-jnp.expm1(2.0 * log_a))
    bterm = mult * ig * uc.astype(F32)
    bterm = bterm.at[:, 0].add(a[:, 0] * h0.astype(F32))
    _, h = lax.associative_scan(_lin_combine, (a, bterm), axis=1)
    y = h.astype(x.dtype) * jax.nn.gelu(gate_br)
    out = jnp.einsum('bse,ed->bsd', y, w_out)
    return out, h[:, -1].astype(h0.dtype), conv_new


def pool_mixer(x, prefix, pos0, w_map, scale):
    s = x.shape[1]
    ext = jnp.concatenate([prefix.astype(x.dtype), x], axis=1)
    cs = jnp.pad(jnp.cumsum(ext.astype(F32), axis=1), ((0, 0), (1, 0), (0, 0)))
    pos = pos0 + jnp.arange(s, dtype=F32)
    upper = cs[:, POOL_STATE + 1:]
    outs = []
    for g, w in enumerate(POOL_WINDOWS):
        lo, hi = g * POOL_GROUP_DIM, (g + 1) * POOL_GROUP_DIM
        lower = cs[:, POOL_STATE + 1 - w:POOL_STATE + 1 - w + s, lo:hi]
        count = jnp.minimum(float(w), pos + 1.0)[None, :, None]
        mean = (upper[..., lo:hi] - lower) / count
        diff = (mean - x[..., lo:hi].astype(F32)).astype(x.dtype)
        outs.append(jnp.einsum('bsi,ij->bsj', diff, w_map[g]))
    y = jnp.concatenate(outs, axis=-1) * scale
    return y, ext[:, -POOL_STATE:]


def hier_moe(x, wg, bg, we, be, w_gate_up, w_down):
    bsz, s, d = x.shape
    xt = x.reshape(-1, d)
    t = xt.shape[0]
    lg = (xt @ wg + bg).astype(F32)
    pg = jax.nn.softmax(lg, axis=-1)
    gsel = jnp.argmax(lg, axis=-1)
    p_sel = jnp.max(pg, axis=-1, keepdims=True)
    le = (xt @ we + be).astype(F32).reshape(t, MOE_GROUPS, MOE_EXPERTS_PER_GROUP)
    le_sel = le[jnp.arange(t), gsel]
    topv, topi = lax.top_k(le_sel, MOE_TOP_K)
    wt = jax.nn.softmax(topv, axis=-1) * p_sel
    eidx = gsel[:, None] * MOE_EXPERTS_PER_GROUP + topi
    gates = jnp.sum(jax.nn.one_hot(eidx, N_EXPERTS, dtype=F32) * wt[..., None], axis=1)
    hu = jnp.einsum('td,edf->tef', xt, w_gate_up)
    hg, hv = jnp.split(hu, 2, axis=-1)
    act = jax.nn.silu(hg) * hv * gates[..., None].astype(x.dtype)
    out = jnp.einsum('tef,efd->td', act, w_down)
    return out.reshape(bsz, s, d)


def run_trunk(x, lru_h0, lru_conv0, pool0, pos0,
              lru_w_in, lru_conv_w, lru_conv_b, lru_w_a, lru_b_a, lru_w_x, lru_b_x, lru_lambda, lru_w_out,
              pool_w, pool_scale, ln_g, ln_b,
              moe_router_g_w, moe_router_g_b, moe_router_e_w, moe_router_e_b, moe_w_gate_up, moe_w_down):
    new_h, new_conv, new_pool = [], [], []
    for i in range(DEPTH):
        j = i // N_MIXERS
        if i % N_MIXERS == 0:
            mix, h, c = rg_lru_block(x, lru_h0[j], lru_conv0[j], lru_w_in[j], lru_conv_w[j], lru_conv_b[j],
                                     lru_w_a[j], lru_b_a[j], lru_w_x[j], lru_b_x[j], lru_lambda[j], lru_w_out[j])
            new_h.append(h)
            new_conv.append(c)
        else:
            mix, p = pool_mixer(x, pool0[j], pos0, pool_w[j], pool_scale[j])
            new_pool.append(p)
        x = layer_norm(DEEPNORM_ALPHA * x + mix, ln_g[i, 0], ln_b[i, 0])
        ffn = hier_moe(x, moe_router_g_w[i], moe_router_g_b[i], moe_router_e_w[i], moe_router_e_b[i],
                       moe_w_gate_up[i], moe_w_down[i])
        x = layer_norm(DEEPNORM_ALPHA * x + ffn, ln_g[i, 1], ln_b[i, 1])
    return x, jnp.stack(new_h), jnp.stack(new_conv), jnp.stack(new_pool)


def setup_inputs(seed: int = 0) -> dict:
    key = jax.random.key(seed)
    ks = jax.random.split(key, 24)
    nrm = jax.random.normal
    s_lam = jax.random.uniform(ks[10], (N_LRU_LAYERS, D_RNN), minval=0.9, maxval=0.999)
    return {
        "x_prompt": nrm(ks[0], (BATCH, SEQ, D_MODEL), F32),
        "x_sample": nrm(ks[1], (DEC_BATCH, DEC_SEQ, D_MODEL), F32),
        "state_lru_h": 0.3 * nrm(ks[2], (N_LRU_LAYERS, DEC_BATCH, D_RNN), F32),
        "state_lru_conv": nrm(ks[3], (N_LRU_LAYERS, DEC_BATCH, CONV_WIDTH - 1, D_RNN), F32),
        "state_pool": nrm(ks[4], (N_POOL_LAYERS, DEC_BATCH, POOL_STATE, D_MODEL), F32),
        "lru_w_in": nrm(ks[5], (N_LRU_LAYERS, D_MODEL, 2 * D_RNN), F32) * D_MODEL ** -0.5,
        "lru_conv_w": nrm(ks[6], (N_LRU_LAYERS, CONV_WIDTH, D_RNN), F32) * CONV_WIDTH ** -0.5,
        "lru_conv_b": 0.02 * nrm(ks[7], (N_LRU_LAYERS, D_RNN), F32),
        "lru_w_a": nrm(ks[8], (N_LRU_LAYERS, LRU_HEADS, LRU_BLOCK, LRU_BLOCK), F32) * LRU_BLOCK ** -0.5,
        "lru_b_a": 0.02 * nrm(ks[9], (N_LRU_LAYERS, LRU_HEADS, LRU_BLOCK), F32),
        "lru_w_x": nrm(ks[11], (N_LRU_LAYERS, LRU_HEADS, LRU_BLOCK, LRU_BLOCK), F32) * LRU_BLOCK ** -0.5,
        "lru_b_x": 0.02 * nrm(ks[12], (N_LRU_LAYERS, LRU_HEADS, LRU_BLOCK), F32),
        "lru_lambda": jnp.log(s_lam) - jnp.log1p(-s_lam),
        "lru_w_out": nrm(ks[13], (N_LRU_LAYERS, D_RNN, D_MODEL), F32) * (D_RNN ** -0.5 * DEEPNORM_BETA),
        "pool_w": nrm(ks[14], (N_POOL_LAYERS, POOL_GROUPS, POOL_GROUP_DIM, POOL_GROUP_DIM), F32) * (POOL_GROUP_DIM ** -0.5 * DEEPNORM_BETA),
        "pool_scale": 1.0 + 0.02 * nrm(ks[15], (N_POOL_LAYERS, D_MODEL), F32),
        "ln_g": 1.0 + 0.02 * nrm(ks[16], (DEPTH, 2, D_MODEL), F32),
        "ln_b": 0.02 * nrm(ks[17], (DEPTH, 2, D_MODEL), F32),
        "moe_router_g_w": nrm(ks[18], (DEPTH, D_MODEL, MOE_GROUPS), F32) * D_MODEL ** -0.5,
        "moe_router_g_b": 0.01 * nrm(ks[19], (DEPTH, MOE_GROUPS), F32),
        "moe_router_e_w": nrm(ks[20], (DEPTH, D_MODEL, N_EXPERTS), F32) * D_MODEL ** -0.5,
        "moe_router_e_b": 0.01 * nrm(ks[21], (DEPTH, N_EXPERTS), F32),
        "moe_w_gate_up": nrm(ks[22], (DEPTH, N_EXPERTS, D_MODEL, 2 * D_EXPERT), F32) * D_MODEL ** -0.5,
        "moe_w_down": nrm(ks[23], (DEPTH, N_EXPERTS, D_EXPERT, D_MODEL), F32) * (D_EXPERT ** -0.5 * DEEPNORM_BETA),
    }


def reference(x_prompt, x_sample, state_lru_h, state_lru_conv, state_pool,
              lru_w_in, lru_conv_w, lru_conv_b, lru_w_a, lru_b_a, lru_w_x, lru_b_x, lru_lambda, lru_w_out,
              pool_w, pool_scale, ln_g, ln_b,
              moe_router_g_w, moe_router_g_b, moe_router_e_w, moe_router_e_b, moe_w_gate_up, moe_w_down):
    bp = x_prompt.shape[0]
    dt = x_prompt.dtype
    h0_p = jnp.zeros((N_LRU_LAYERS, bp, D_RNN), dt)
    conv0_p = jnp.zeros((N_LRU_LAYERS, bp, CONV_WIDTH - 1, D_RNN), dt)
    pool0_p = jnp.zeros((N_POOL_LAYERS, bp, POOL_STATE, D_MODEL), dt)
    y_prompt, lru_h_prompt, lru_conv_prompt, pool_prompt = run_trunk(
        x_prompt, h0_p, conv0_p, pool0_p, 0,
        lru_w_in, lru_conv_w, lru_conv_b, lru_w_a, lru_b_a, lru_w_x, lru_b_x, lru_lambda, lru_w_out,
        pool_w, pool_scale, ln_g, ln_b,
        moe_router_g_w, moe_router_g_b, moe_router_e_w, moe_router_e_b, moe_w_gate_up, moe_w_down)
    y_sample, lru_h_sample, lru_conv_sample, pool_sample = run_trunk(
        x_sample, state_lru_h, state_lru_conv, state_pool, PAST_LEN,
        lru_w_in, lru_conv_w, lru_conv_b, lru_w_a, lru_b_a, lru_w_x, lru_b_x, lru_lambda, lru_w_out,
        pool_w, pool_scale, ln_g, ln_b,
        moe_router_g_w, moe_router_g_b, moe_router_e_w, moe_router_e_b, moe_w_gate_up, moe_w_down)
    return (y_prompt, y_sample, lru_h_prompt, lru_conv_prompt, pool_prompt,
            lru_h_sample, lru_conv_sample, pool_sample)
```

```python
import functools

import jax
import jax.numpy as jnp
from jax import lax
from jax.experimental import pallas as pl
from jax.experimental.pallas import tpu as pltpu

F32 = jnp.float32
BF16 = jnp.bfloat16

D_MODEL = 2048
DEPTH = 2
D_RNN = D_MODEL
LRU_HEADS = 8
LRU_BLOCK = D_RNN // LRU_HEADS
CONV_WIDTH = 4
LRU_C = 8.0
POOL_WINDOWS = (2, 4, 8, 16)
POOL_GROUPS = 4
POOL_GROUP_DIM = D_MODEL // POOL_GROUPS
POOL_STATE = max(POOL_WINDOWS) - 1
MOE_GROUPS = 4
MOE_EXPERTS_PER_GROUP = 4
N_EXPERTS = MOE_GROUPS * MOE_EXPERTS_PER_GROUP
D_EXPERT = 512
DEEPNORM_ALPHA = (2.0 * DEPTH) ** 0.25
LN_EPS = 1e-5
PAST_LEN = 2048

LANES = 128
SUBLANES = 8
ROUTER_LANES = LANES
EXPERT_LANE0 = MOE_GROUPS
VMEM_LIMIT = 56 * 1024 * 1024


def _cparams(*sem):
    return pltpu.CompilerParams(dimension_semantics=sem, vmem_limit_bytes=VMEM_LIMIT)


def _layer_norm(z, g, b):
    mu = jnp.mean(z, axis=-1, keepdims=True)
    zc = z - mu
    var = jnp.mean(zc * zc, axis=-1, keepdims=True)
    return zc * lax.rsqrt(var + LN_EPS) * g + b


def _route(x1, wr_a_ref, wr_b_ref, rb_ref):
    hi = x1.astype(BF16)
    lo = (x1 - hi.astype(F32)).astype(BF16)
    c = jnp.dot(hi, wr_a_ref[...], preferred_element_type=F32)
    d = jnp.dot(lo, wr_b_ref[...], preferred_element_type=F32)
    logits = c[:, :ROUTER_LANES] + c[:, ROUTER_LANES:] + d + rb_ref[...]
    tm = logits.shape[0]
    lane = lax.broadcasted_iota(jnp.int32, (tm, ROUTER_LANES), 1)
    neg = jnp.float32(-jnp.inf)
    is_g = lane < MOE_GROUPS
    lg = jnp.where(is_g, logits, neg)
    gmax = jnp.max(lg, axis=1, keepdims=True)
    gsel = jnp.min(jnp.where(lg == gmax, lane, ROUTER_LANES), axis=1, keepdims=True)
    p_sel = 1.0 / jnp.sum(jnp.where(is_g, jnp.exp(logits - gmax), 0.0), axis=1, keepdims=True)
    elane = lane - EXPERT_LANE0
    in_g = (elane >= 0) & (elane < N_EXPERTS) & ((elane >> 2) == gsel)
    le = jnp.where(in_g, logits, neg)
    t1 = jnp.max(le, axis=1, keepdims=True)
    i1 = jnp.min(jnp.where(le == t1, lane, ROUTER_LANES), axis=1, keepdims=True)
    le2 = jnp.where(lane == i1, neg, le)
    t2 = jnp.max(le2, axis=1, keepdims=True)
    i2 = jnp.min(jnp.where(le2 == t2, lane, ROUTER_LANES), axis=1, keepdims=True)
    e2 = jnp.exp(t2 - t1)
    inv = p_sel / (1.0 + e2)
    gates = jnp.where(lane == i1, inv, 0.0) + jnp.where(lane == i2, inv * e2, 0.0)
    return hi, gates


def _proj_kernel(x_ref, w_ref, o_ref):
    o_ref[...] = jnp.dot(x_ref[...].astype(BF16), w_ref[...], preferred_element_type=F32)


def _proj(x, w, tm, tn):
    t, k = x.shape
    n = w.shape[1]
    return pl.pallas_call(
        _proj_kernel,
        grid=(n // tn, t // tm),
        in_specs=[pl.BlockSpec((tm, k), lambda jn, im: (im, 0)),
                  pl.BlockSpec((k, tn), lambda jn, im: (0, jn))],
        out_specs=pl.BlockSpec((tm, tn), lambda jn, im: (im, jn)),
        out_shape=jax.ShapeDtypeStruct((t, n), F32),
        compiler_params=_cparams("arbitrary", "arbitrary"),
        name="lru_in_proj",
    )(x, w)


def _lru_kernel(gate_ref, u_ref, h0_ref, conv0_ref, cw_ref, cb_ref, wa_ref, ba_ref, wx_ref, bx_ref,
                lam_ref, y_ref, hl_ref, cn_ref, ubuf, hc, *, ts):
    j = pl.program_id(1)

    @pl.when(j == 0)
    def _():
        ubuf[0:SUBLANES, :] = jnp.zeros((SUBLANES, D_RNN), F32)
        ubuf[SUBLANES - (CONV_WIDTH - 1):SUBLANES, :] = conv0_ref[...]
        hc[...] = h0_ref[...]

    u = u_ref[...]
    ubuf[SUBLANES:SUBLANES + ts, :] = u
    uc = cb_ref[...] + ubuf[SUBLANES - 3:SUBLANES - 3 + ts, :] * cw_ref[0:1, :]
    uc = uc + ubuf[SUBLANES - 2:SUBLANES - 2 + ts, :] * cw_ref[1:2, :]
    uc = uc + ubuf[SUBLANES - 1:SUBLANES - 1 + ts, :] * cw_ref[2:3, :]
    uc = uc + u * cw_ref[3:4, :]
    cn_ref[...] = ubuf[SUBLANES + ts - (CONV_WIDTH - 1):SUBLANES + ts, :]
    ubuf[0:SUBLANES, :] = ubuf[ts:ts + SUBLANES, :]

    ucb = uc.astype(BF16)
    rs, igs = [], []
    for h in range(LRU_HEADS):
        blk = ucb[:, h * LRU_BLOCK:(h + 1) * LRU_BLOCK]
        rs.append(jnp.dot(blk, wa_ref[h], preferred_element_type=F32))
        igs.append(jnp.dot(blk, wx_ref[h], preferred_element_type=F32))
    r = jax.nn.sigmoid(jnp.concatenate(rs, axis=1) + ba_ref[...])
    ig = jax.nn.sigmoid(jnp.concatenate(igs, axis=1) + bx_ref[...])
    log_a = (-LRU_C) * r * jax.nn.softplus(-lam_ref[...])
    a = jnp.exp(log_a)
    b = jnp.sqrt(-jnp.tanh(log_a) * (a * a + 1.0)) * ig * uc

    row = lax.broadcasted_iota(jnp.int32, (ts, D_RNN), 0) & (SUBLANES - 1)
    for s in (1, 2, 4):
        keep = row >= s
        a_s = pltpu.roll(a, s, 0)
        b_s = pltpu.roll(b, s, 0)
        b = jnp.where(keep, a * b_s + b, b)
        a = jnp.where(keep, a * a_s, a)
    hprev = hc[...]
    hs = []
    for g in range(ts // SUBLANES):
        hg = a[g * SUBLANES:(g + 1) * SUBLANES] * hprev + b[g * SUBLANES:(g + 1) * SUBLANES]
        hs.append(hg)
        hprev = hg[SUBLANES - 1:SUBLANES]
    h = jnp.concatenate(hs, axis=0)
    hc[...] = hprev
    hl_ref[...] = hprev
    y_ref[...] = (h * jax.nn.gelu(gate_ref[...])).astype(BF16)


def _lru_scan(proj, h0, conv0, cw, cb, wa, ba, wx, bx, lam, ts):
    bsz, s, _ = proj.shape
    row2 = lambda b, j: (0, 0)
    return pl.pallas_call(
        functools.partial(_lru_kernel, ts=ts),
        grid=(bsz, s // ts),
        in_specs=[pl.BlockSpec((None, ts, D_RNN), lambda b, j: (b, j, 0)),
                  pl.BlockSpec((None, ts, D_RNN), lambda b, j: (b, j, 1)),
                  pl.BlockSpec((None, 1, D_RNN), lambda b, j: (b, 0, 0)),
                  pl.BlockSpec((None, CONV_WIDTH - 1, D_RNN), lambda b, j: (b, 0, 0)),
                  pl.BlockSpec((CONV_WIDTH, D_RNN), row2),
                  pl.BlockSpec((1, D_RNN), row2),
                  pl.BlockSpec((LRU_HEADS, LRU_BLOCK, LRU_BLOCK), lambda b, j: (0, 0, 0)),
                  pl.BlockSpec((1, D_RNN), row2),
                  pl.BlockSpec((LRU_HEADS, LRU_BLOCK, LRU_BLOCK), lambda b, j: (0, 0, 0)),
                  pl.BlockSpec((1, D_RNN), row2),
                  pl.BlockSpec((1, D_RNN), row2)],
        out_specs=[pl.BlockSpec((None, ts, D_RNN), lambda b, j: (b, j, 0)),
                   pl.BlockSpec((None, 1, D_RNN), lambda b, j: (b, 0, 0)),
                   pl.BlockSpec((None, CONV_WIDTH - 1, D_RNN), lambda b, j: (b, 0, 0))],
        out_shape=[jax.ShapeDtypeStruct((bsz, s, D_RNN), BF16),
                   jax.ShapeDtypeStruct((bsz, 1, D_RNN), F32),
                   jax.ShapeDtypeStruct((bsz, CONV_WIDTH - 1, D_RNN), F32)],
        scratch_shapes=[pltpu.VMEM((SUBLANES + ts, D_RNN), F32), pltpu.VMEM((1, D_RNN), F32)],
        compiler_params=_cparams("arbitrary", "arbitrary"),
        name="lru_scan",
    )(proj, proj, h0, conv0, cw, cb, wa, ba, wx, bx, lam)


def _out_kernel(y_ref, w_ref, x_ref, g_ref, b_ref, wr_a_ref, wr_b_ref, rb_ref, x1_ref, x1b_ref, gates_ref):
    mix = jnp.dot(y_ref[...], w_ref[...], preferred_element_type=F32)
    x1 = _layer_norm(DEEPNORM_ALPHA * x_ref[...] + mix, g_ref[...], b_ref[...])
    x1_ref[...] = x1
    x1b_ref[...], gates_ref[...] = _route(x1, wr_a_ref, wr_b_ref, rb_ref)


def _out_proj(y, w, x, g, b, wr_a, wr_b, rb, tm):
    t = x.shape[0]
    full = lambda i: (0, 0)
    tile = lambda i: (i, 0)
    return pl.pallas_call(
        _out_kernel,
        grid=(t // tm,),
        in_specs=[pl.BlockSpec((tm, D_RNN), tile),
                  pl.BlockSpec((D_RNN, D_MODEL), full),
                  pl.BlockSpec((tm, D_MODEL), tile),
                  pl.BlockSpec((1, D_MODEL), full),
                  pl.BlockSpec((1, D_MODEL), full),
                  pl.BlockSpec((D_MODEL, 2 * ROUTER_LANES), full),
                  pl.BlockSpec((D_MODEL, ROUTER_LANES), full),
                  pl.BlockSpec((1, ROUTER_LANES), full)],
        out_specs=[pl.BlockSpec((tm, D_MODEL), tile),
                   pl.BlockSpec((tm, D_MODEL), tile),
                   pl.BlockSpec((tm, ROUTER_LANES), tile)],
        out_shape=[jax.ShapeDtypeStruct((t, D_MODEL), F32),
                   jax.ShapeDtypeStruct((t, D_MODEL), BF16),
                   jax.ShapeDtypeStruct((t, ROUTER_LANES), F32)],
        compiler_params=_cparams("arbitrary"),
        name="lru_out_ln_route",
    )(y, w, x, g, b, wr_a, wr_b, rb)


def _pool_kernel(x_ref, p0_ref, pw_ref, sc_ref, g_ref, b_ref, wr_a_ref, wr_b_ref, rb_ref,
                 x1_ref, x1b_ref, gates_ref, pn_ref, ext, *, ts, pos0):
    j = pl.program_id(1)
    hist = POOL_STATE + 1

    @pl.when(j == 0)
    def _():
        ext[0:hist, :] = p0_ref[...]

    x = x_ref[...]
    ext[hist:hist + ts, :] = x
    pos = (pos0 + j * ts + lax.broadcasted_iota(jnp.int32, (ts, 1), 0)).astype(F32)
    outs = []
    for g, w in enumerate(POOL_WINDOWS):
        lo, hi = g * POOL_GROUP_DIM, (g + 1) * POOL_GROUP_DIM
        acc = x[:, lo:hi]
        for k in range(1, w):
            acc = acc + ext[hist - k:hist - k + ts, lo:hi]
        mean = acc / jnp.minimum(jnp.float32(w), pos + 1.0)
        diff = (mean - x[:, lo:hi]).astype(BF16)
        outs.append(jnp.dot(diff, pw_ref[g], preferred_element_type=F32))
    mix = jnp.concatenate(outs, axis=1) * sc_ref[...]
    pn_ref[...] = ext[ts:ts + hist, :]
    ext[0:hist, :] = ext[ts:ts + hist, :]
    x1 = _layer_norm(DEEPNORM_ALPHA * x + mix, g_ref[...], b_ref[...])
    x1_ref[...] = x1
    x1b_ref[...], gates_ref[...] = _route(x1, wr_a_ref, wr_b_ref, rb_ref)


def _pool_mix(x, p0, pw, sc, g, b, wr_a, wr_b, rb, ts, pos0):
    bsz, s, _ = x.shape
    hist = POOL_STATE + 1
    full = lambda bb, j: (0, 0)
    tile = lambda bb, j: (bb, j, 0)
    return pl.pallas_call(
        functools.partial(_pool_kernel, ts=ts, pos0=pos0),
        grid=(bsz, s // ts),
        in_specs=[pl.BlockSpec((None, ts, D_MODEL), tile),
                  pl.BlockSpec((None, hist, D_MODEL), lambda bb, j: (bb, 0, 0)),
                  pl.BlockSpec((POOL_GROUPS, POOL_GROUP_DIM, POOL_GROUP_DIM), lambda bb, j: (0, 0, 0)),
                  pl.BlockSpec((1, D_MODEL), full),
                  pl.BlockSpec((1, D_MODEL), full),
                  pl.BlockSpec((1, D_MODEL), full),
                  pl.BlockSpec((D_MODEL, 2 * ROUTER_LANES), full),
                  pl.BlockSpec((D_MODEL, ROUTER_LANES), full),
                  pl.BlockSpec((1, ROUTER_LANES), full)],
        out_specs=[pl.BlockSpec((None, ts, D_MODEL), tile),
                   pl.BlockSpec((None, ts, D_MODEL), tile),
                   pl.BlockSpec((None, ts, ROUTER_LANES), tile),
                   pl.BlockSpec((None, hist, D_MODEL), lambda bb, j: (bb, 0, 0))],
        out_shape=[jax.ShapeDtypeStruct((bsz, s, D_MODEL), F32),
                   jax.ShapeDtypeStruct((bsz, s, D_MODEL), BF16),
                   jax.ShapeDtypeStruct((bsz, s, ROUTER_LANES), F32),
                   jax.ShapeDtypeStruct((bsz, hist, D_MODEL), F32)],
        scratch_shapes=[pltpu.VMEM((hist + ts, D_MODEL), F32)],
        compiler_params=_cparams("arbitrary", "arbitrary"),
        name="pool_mix_ln_route",
    )(x, p0, pw, sc, g, b, wr_a, wr_b, rb)


def _moe_kernel(xb_ref, gates_ref, wgu_ref, wd_ref, x1_ref, g_ref, b_ref, o_ref, acc):
    e = pl.program_id(1)

    @pl.when(e == 0)
    def _():
        acc[...] = jnp.zeros_like(acc)

    hu = jnp.dot(xb_ref[...], wgu_ref[...], preferred_element_type=F32)
    hg, hv = hu[:, :D_EXPERT], hu[:, D_EXPERT:]
    gates = gates_ref[...]
    lane = lax.broadcasted_iota(jnp.int32, gates.shape, 1)
    gate = jnp.sum(jnp.where(lane == e + EXPERT_LANE0, gates, 0.0), axis=1, keepdims=True)
    act = (jax.nn.silu(hg) * hv * gate).astype(BF16)
    acc[...] += jnp.dot(act, wd_ref[...], preferred_element_type=F32)

    @pl.when(e == N_EXPERTS - 1)
    def _():
        o_ref[...] = _layer_norm(DEEPNORM_ALPHA * x1_ref[...] + acc[...], g_ref[...], b_ref[...])


def _moe(xb, gates, wgu, wd, x1, g, b, tm):
    t = xb.shape[0]
    tile = lambda i, e: (i, 0)
    full = lambda i, e: (0, 0)
    return pl.pallas_call(
        _moe_kernel,
        grid=(t // tm, N_EXPERTS),
        in_specs=[pl.BlockSpec((tm, D_MODEL), tile),
                  pl.BlockSpec((tm, ROUTER_LANES), tile),
                  pl.BlockSpec((None, D_MODEL, 2 * D_EXPERT), lambda i, e: (e, 0, 0)),
                  pl.BlockSpec((None, D_EXPERT, D_MODEL), lambda i, e: (e, 0, 0)),
                  pl.BlockSpec((tm, D_MODEL), tile),
                  pl.BlockSpec((1, D_MODEL), full),
                  pl.BlockSpec((1, D_MODEL), full)],
        out_specs=pl.BlockSpec((tm, D_MODEL), tile),
        out_shape=jax.ShapeDtypeStruct((t, D_MODEL), F32),
        scratch_shapes=[pltpu.VMEM((tm, D_MODEL), F32)],
        compiler_params=_cparams("arbitrary", "arbitrary"),
        name="moe_ffn_ln",
    )(xb, gates, wgu, wd, x1, g, b)


def _router_weights(wg, bg, we, be):
    w = jnp.concatenate([wg, we], axis=1)
    w = jnp.pad(w, ((0, 0), (0, ROUTER_LANES - w.shape[1])))
    w_hi = w.astype(BF16)
    w_lo = (w - w_hi.astype(F32)).astype(BF16)
    rb = jnp.pad(jnp.concatenate([bg, be]), (0, ROUTER_LANES - MOE_GROUPS - N_EXPERTS))
    return jnp.concatenate([w_hi, w_lo], axis=1), w_hi, rb.reshape(1, ROUTER_LANES)


def _trunk(x, lru_h0, lru_conv0, pool0, pos0, wts, ts, tm):
    bsz, s, d = x.shape
    t = bsz * s
    row = lambda v: v.reshape(1, -1)
    proj = _proj(x.reshape(t, d), wts["w_in"], min(512, t), 1024).reshape(bsz, s, 2 * D_RNN)
    y, h_last, conv_new = _lru_scan(
        proj, lru_h0.reshape(bsz, 1, D_RNN), lru_conv0, wts["conv_w"], row(wts["conv_b"]),
        wts["w_a"], row(wts["b_a"]), wts["w_x"], row(wts["b_x"]), row(wts["lam"]), ts)
    x1, x1b, gates = _out_proj(y.reshape(t, D_RNN), wts["w_out"], x.reshape(t, d),
                               row(wts["ln_g"][0, 0]), row(wts["ln_b"][0, 0]), *wts["router"][0], tm)
    x2 = _moe(x1b, gates, wts["wgu"][0], wts["wd"][0], x1,
              row(wts["ln_g"][0, 1]), row(wts["ln_b"][0, 1]), min(512, t))
    p0 = jnp.pad(pool0, ((0, 0), (1, 0), (0, 0)))
    x1, x1b, gates, pool_new = _pool_mix(
        x2.reshape(bsz, s, d), p0, wts["pool_w"], row(wts["pool_scale"]),
        row(wts["ln_g"][1, 0]), row(wts["ln_b"][1, 0]), *wts["router"][1], ts, pos0)
    y_out = _moe(x1b.reshape(t, d), gates.reshape(t, ROUTER_LANES), wts["wgu"][1], wts["wd"][1],
                 x1.reshape(t, d), row(wts["ln_g"][1, 1]), row(wts["ln_b"][1, 1]), min(512, t))
    return (y_out.reshape(bsz, s, d), h_last.reshape(1, bsz, D_RNN), conv_new[None],
            pool_new[None, :, 1:, :])


def kernel(x_prompt, x_sample, state_lru_h, state_lru_conv, state_pool, lru_w_in, lru_conv_w, lru_conv_b, lru_w_a, lru_b_a, lru_w_x, lru_b_x, lru_lambda, lru_w_out, pool_w, pool_scale, ln_g, ln_b, moe_router_g_w, moe_router_g_b, moe_router_e_w, moe_router_e_b, moe_w_gate_up, moe_w_down):
    wts = {
        "w_in": lru_w_in[0].astype(BF16), "conv_w": lru_conv_w[0], "conv_b": lru_conv_b[0],
        "w_a": lru_w_a[0].astype(BF16), "b_a": lru_b_a[0], "w_x": lru_w_x[0].astype(BF16), "b_x": lru_b_x[0],
        "lam": lru_lambda[0], "w_out": lru_w_out[0].astype(BF16),
        "pool_w": pool_w[0].astype(BF16), "pool_scale": pool_scale[0], "ln_g": ln_g, "ln_b": ln_b,
        "router": [_router_weights(moe_router_g_w[i], moe_router_g_b[i], moe_router_e_w[i], moe_router_e_b[i])
                   for i in range(DEPTH)],
        "wgu": moe_w_gate_up.astype(BF16), "wd": moe_w_down.astype(BF16),
    }
    bp = x_prompt.shape[0]
    dt = x_prompt.dtype
    y_p, h_p, c_p, p_p = _trunk(
        x_prompt, jnp.zeros((bp, D_RNN), dt), jnp.zeros((bp, CONV_WIDTH - 1, D_RNN), dt),
        jnp.zeros((bp, POOL_STATE, D_MODEL), dt), 0, wts, ts=256, tm=256)
    y_s, h_s, c_s, p_s = _trunk(
        x_sample, state_lru_h[0], state_lru_conv[0], state_pool[0], PAST_LEN, wts,
        ts=x_sample.shape[1], tm=256)
    return (y_p, y_s, h_p, c_p, p_p, h_s, c_s, p_s)
```

```python
import functools

import jax
import jax.numpy as jnp
from jax import lax
from jax.experimental import pallas as pl
from jax.experimental.pallas import tpu as pltpu

F32 = jnp.float32
BF16 = jnp.bfloat16

D_MODEL = 2048
DEPTH = 2
D_RNN = D_MODEL
LRU_HEADS = 8
LRU_BLOCK = D_RNN // LRU_HEADS
CONV_WIDTH = 4
LRU_C = 8.0
POOL_WINDOWS = (2, 4, 8, 16)
POOL_GROUPS = 4
POOL_GROUP_DIM = D_MODEL // POOL_GROUPS
POOL_STATE = max(POOL_WINDOWS) - 1
MOE_GROUPS = 4
MOE_EXPERTS_PER_GROUP = 4
N_EXPERTS = MOE_GROUPS * MOE_EXPERTS_PER_GROUP
D_EXPERT = 512
DEEPNORM_ALPHA = (2.0 * DEPTH) ** 0.25
LN_EPS = 1e-5
PAST_LEN = 2048

LANES = 128
SUBLANES = 8
ROUTER_LANES = LANES
EXPERT_LANE0 = MOE_GROUPS
CLASS_LANE = 0
PAIRS_PER_GROUP = 6
N_CLASSES = MOE_GROUPS * PAIRS_PER_GROUP
PAIR_SLOTS = ((0, 1), (0, 2), (0, 3), (1, 3), (1, 2), (3, 2))
PAIR_ORDER = (0, 1, 2, 4, 3, 5)
VMEM_LIMIT = 56 * 1024 * 1024
MOE_TILE = 256


def _cparams(*sem):
    return pltpu.CompilerParams(dimension_semantics=sem, vmem_limit_bytes=VMEM_LIMIT)


def _layer_norm(z, g, b):
    mu = jnp.mean(z, axis=-1, keepdims=True)
    zc = z - mu
    var = jnp.mean(zc * zc, axis=-1, keepdims=True)
    return zc * lax.rsqrt(var + LN_EPS) * g + b


def _route(x1, wr_a_ref, wr_b_ref, rb_ref):
    hi = x1.astype(BF16)
    lo = (x1 - hi.astype(F32)).astype(BF16)
    c = jnp.dot(hi, wr_a_ref[...], preferred_element_type=F32)
    d = jnp.dot(lo, wr_b_ref[...], preferred_element_type=F32)
    logits = c[:, :ROUTER_LANES] + c[:, ROUTER_LANES:] + d + rb_ref[...]
    tm = logits.shape[0]
    lane = lax.broadcasted_iota(jnp.int32, (tm, ROUTER_LANES), 1)
    neg = jnp.float32(-jnp.inf)
    is_g = lane < MOE_GROUPS
    lg = jnp.where(is_g, logits, neg)
    gmax = jnp.max(lg, axis=1, keepdims=True)
    gsel = jnp.min(jnp.where(lg == gmax, lane, ROUTER_LANES), axis=1, keepdims=True)
    p_sel = 1.0 / jnp.sum(jnp.where(is_g, jnp.exp(logits - gmax), 0.0), axis=1, keepdims=True)
    elane = lane - EXPERT_LANE0
    in_g = (elane >= 0) & (elane < N_EXPERTS) & ((elane >> 2) == gsel)
    le = jnp.where(in_g, logits, neg)
    t1 = jnp.max(le, axis=1, keepdims=True)
    i1 = jnp.min(jnp.where(le == t1, lane, ROUTER_LANES), axis=1, keepdims=True)
    le2 = jnp.where(lane == i1, neg, le)
    t2 = jnp.max(le2, axis=1, keepdims=True)
    i2 = jnp.min(jnp.where(le2 == t2, lane, ROUTER_LANES), axis=1, keepdims=True)
    e2 = jnp.exp(t2 - t1)
    inv = p_sel / (1.0 + e2)
    gates = jnp.where(lane == i1, inv, 0.0) + jnp.where(lane == i2, inv * e2, 0.0)
    base = EXPERT_LANE0 + MOE_EXPERTS_PER_GROUP * gsel
    pa = jnp.minimum(i1, i2) - base
    pb = jnp.maximum(i1, i2) - base
    cls = PAIRS_PER_GROUP * gsel + 3 * pa - ((pa * (pa - 1)) >> 1) + pb - pa - 1
    gates = jnp.where(lane == CLASS_LANE, cls.astype(F32), gates)
    return hi, gates


def _proj_kernel(x_ref, w_ref, o_ref):
    o_ref[...] = jnp.dot(x_ref[...].astype(BF16), w_ref[...], preferred_element_type=F32)


def _proj(x, w, tm, tn):
    t, k = x.shape
    n = w.shape[1]
    return pl.pallas_call(
        _proj_kernel,
        grid=(n // tn, t // tm),
        in_specs=[pl.BlockSpec((tm, k), lambda jn, im: (im, 0)),
                  pl.BlockSpec((k, tn), lambda jn, im: (0, jn))],
        out_specs=pl.BlockSpec((tm, tn), lambda jn, im: (im, jn)),
        out_shape=jax.ShapeDtypeStruct((t, n), F32),
        compiler_params=_cparams("arbitrary", "arbitrary"),
        name="lru_in_proj",
    )(x, w)


def _lru_kernel(gate_ref, u_ref, h0_ref, conv0_ref, cw_ref, cb_ref, wa_ref, ba_ref, wx_ref, bx_ref,
                lam_ref, y_ref, hl_ref, cn_ref, ubuf, hc, *, ts):
    j = pl.program_id(1)

    @pl.when(j == 0)
    def _():
        ubuf[0:SUBLANES, :] = jnp.zeros((SUBLANES, D_RNN), F32)
        ubuf[SUBLANES - (CONV_WIDTH - 1):SUBLANES, :] = conv0_ref[...]
        hc[...] = h0_ref[...]

    u = u_ref[...]
    ubuf[SUBLANES:SUBLANES + ts, :] = u
    uc = cb_ref[...] + ubuf[SUBLANES - 3:SUBLANES - 3 + ts, :] * cw_ref[0:1, :]
    uc = uc + ubuf[SUBLANES - 2:SUBLANES - 2 + ts, :] * cw_ref[1:2, :]
    uc = uc + ubuf[SUBLANES - 1:SUBLANES - 1 + ts, :] * cw_ref[2:3, :]
    uc = uc + u * cw_ref[3:4, :]
    cn_ref[...] = ubuf[SUBLANES + ts - (CONV_WIDTH - 1):SUBLANES + ts, :]
    ubuf[0:SUBLANES, :] = ubuf[ts:ts + SUBLANES, :]

    ucb = uc.astype(BF16)
    rs, igs = [], []
    for h in range(LRU_HEADS):
        blk = ucb[:, h * LRU_BLOCK:(h + 1) * LRU_BLOCK]
        rs.append(jnp.dot(blk, wa_ref[h], preferred_element_type=F32))
        igs.append(jnp.dot(blk, wx_ref[h], preferred_element_type=F32))
    r = jax.nn.sigmoid(jnp.concatenate(rs, axis=1) + ba_ref[...])
    ig = jax.nn.sigmoid(jnp.concatenate(igs, axis=1) + bx_ref[...])
    log_a = (-LRU_C) * r * jax.nn.softplus(-lam_ref[...])
    a = jnp.exp(log_a)
    b = jnp.sqrt(-jnp.tanh(log_a) * (a * a + 1.0)) * ig * uc

    row = lax.broadcasted_iota(jnp.int32, (ts, D_RNN), 0) & (SUBLANES - 1)
    for s in (1, 2, 4):
        keep = row >= s
        a_s = pltpu.roll(a, s, 0)
        b_s = pltpu.roll(b, s, 0)
        b = jnp.where(keep, a * b_s + b, b)
        a = jnp.where(keep, a * a_s, a)
    hprev = hc[...]
    hs = []
    for g in range(ts // SUBLANES):
        hg = a[g * SUBLANES:(g + 1) * SUBLANES] * hprev + b[g * SUBLANES:(g + 1) * SUBLANES]
        hs.append(hg)
        hprev = hg[SUBLANES - 1:SUBLANES]
    h = jnp.concatenate(hs, axis=0)
    hc[...] = hprev
    hl_ref[...] = hprev
    y_ref[...] = (h * jax.nn.gelu(gate_ref[...])).astype(BF16)


def _lru_scan(proj, h0, conv0, cw, cb, wa, ba, wx, bx, lam, ts):
    bsz, s, _ = proj.shape
    row2 = lambda b, j: (0, 0)
    return pl.pallas_call(
        functools.partial(_lru_kernel, ts=ts),
        grid=(bsz, s // ts),
        in_specs=[pl.BlockSpec((None, ts, D_RNN), lambda b, j: (b, j, 0)),
                  pl.BlockSpec((None, ts, D_RNN), lambda b, j: (b, j, 1)),
                  pl.BlockSpec((None, 1, D_RNN), lambda b, j: (b, 0, 0)),
                  pl.BlockSpec((None, CONV_WIDTH - 1, D_RNN), lambda b, j: (b, 0, 0)),
                  pl.BlockSpec((CONV_WIDTH, D_RNN), row2),
                  pl.BlockSpec((1, D_RNN), row2),
                  pl.BlockSpec((LRU_HEADS, LRU_BLOCK, LRU_BLOCK), lambda b, j: (0, 0, 0)),
                  pl.BlockSpec((1, D_RNN), row2),
                  pl.BlockSpec((LRU_HEADS, LRU_BLOCK, LRU_BLOCK), lambda b, j: (0, 0, 0)),
                  pl.BlockSpec((1, D_RNN), row2),
                  pl.BlockSpec((1, D_RNN), row2)],
        out_specs=[pl.BlockSpec((None, ts, D_RNN), lambda b, j: (b, j, 0)),
                   pl.BlockSpec((None, 1, D_RNN), lambda b, j: (b, 0, 0)),
                   pl.BlockSpec((None, CONV_WIDTH - 1, D_RNN), lambda b, j: (b, 0, 0))],
        out_shape=[jax.ShapeDtypeStruct((bsz, s, D_RNN), BF16),
                   jax.ShapeDtypeStruct((bsz, 1, D_RNN), F32),
                   jax.ShapeDtypeStruct((bsz, CONV_WIDTH - 1, D_RNN), F32)],
        scratch_shapes=[pltpu.VMEM((SUBLANES + ts, D_RNN), F32), pltpu.VMEM((1, D_RNN), F32)],
        compiler_params=_cparams("arbitrary", "arbitrary"),
        name="lru_scan",
    )(proj, proj, h0, conv0, cw, cb, wa, ba, wx, bx, lam)


def _out_kernel(y_ref, w_ref, x_ref, g_ref, b_ref, wr_a_ref, wr_b_ref, rb_ref, x1_ref, x1b_ref, gates_ref):
    mix = jnp.dot(y_ref[...], w_ref[...], preferred_element_type=F32)
    x1 = _layer_norm(DEEPNORM_ALPHA * x_ref[...] + mix, g_ref[...], b_ref[...])
    x1_ref[...] = x1
    x1b_ref[...], gates_ref[...] = _route(x1, wr_a_ref, wr_b_ref, rb_ref)


def _out_proj(y, w, x, g, b, wr_a, wr_b, rb, tm):
    t = x.shape[0]
    full = lambda i: (0, 0)
    tile = lambda i: (i, 0)
    return pl.pallas_call(
        _out_kernel,
        grid=(t // tm,),
        in_specs=[pl.BlockSpec((tm, D_RNN), tile),
                  pl.BlockSpec((D_RNN, D_MODEL), full),
                  pl.BlockSpec((tm, D_MODEL), tile),
                  pl.BlockSpec((1, D_MODEL), full),
                  pl.BlockSpec((1, D_MODEL), full),
                  pl.BlockSpec((D_MODEL, 2 * ROUTER_LANES), full),
                  pl.BlockSpec((D_MODEL, ROUTER_LANES), full),
                  pl.BlockSpec((1, ROUTER_LANES), full)],
        out_specs=[pl.BlockSpec((tm, D_MODEL), tile),
                   pl.BlockSpec((tm, D_MODEL), tile),
                   pl.BlockSpec((tm, ROUTER_LANES), tile)],
        out_shape=[jax.ShapeDtypeStruct((t, D_MODEL), F32),
                   jax.ShapeDtypeStruct((t, D_MODEL), BF16),
                   jax.ShapeDtypeStruct((t, ROUTER_LANES), F32)],
        compiler_params=_cparams("arbitrary"),
        name="lru_out_ln_route",
    )(y, w, x, g, b, wr_a, wr_b, rb)


def _pool_kernel(x_ref, p0_ref, pw_ref, sc_ref, g_ref, b_ref, wr_a_ref, wr_b_ref, rb_ref,
                 x1_ref, x1b_ref, gates_ref, pn_ref, ext, *, ts, pos0):
    j = pl.program_id(1)
    hist = POOL_STATE + 1

    @pl.when(j == 0)
    def _():
        ext[0:hist, :] = p0_ref[...]

    x = x_ref[...]
    ext[hist:hist + ts, :] = x
    pos = (pos0 + j * ts + lax.broadcasted_iota(jnp.int32, (ts, 1), 0)).astype(F32)
    outs = []
    for g, w in enumerate(POOL_WINDOWS):
        lo, hi = g * POOL_GROUP_DIM, (g + 1) * POOL_GROUP_DIM
        acc = x[:, lo:hi]
        for k in range(1, w):
            acc = acc + ext[hist - k:hist - k + ts, lo:hi]
        mean = acc / jnp.minimum(jnp.float32(w), pos + 1.0)
        diff = (mean - x[:, lo:hi]).astype(BF16)
        outs.append(jnp.dot(diff, pw_ref[g], preferred_element_type=F32))
    mix = jnp.concatenate(outs, axis=1) * sc_ref[...]
    pn_ref[...] = ext[ts:ts + hist, :]
    ext[0:hist, :] = ext[ts:ts + hist, :]
    x1 = _layer_norm(DEEPNORM_ALPHA * x + mix, g_ref[...], b_ref[...])
    x1_ref[...] = x1
    x1b_ref[...], gates_ref[...] = _route(x1, wr_a_ref, wr_b_ref, rb_ref)


def _pool_mix(x, p0, pw, sc, g, b, wr_a, wr_b, rb, ts, pos0):
    bsz, s, _ = x.shape
    hist = POOL_STATE + 1
    full = lambda bb, j: (0, 0)
    tile = lambda bb, j: (bb, j, 0)
    return pl.pallas_call(
        functools.partial(_pool_kernel, ts=ts, pos0=pos0),
        grid=(bsz, s // ts),
        in_specs=[pl.BlockSpec((None, ts, D_MODEL), tile),
                  pl.BlockSpec((None, hist, D_MODEL), lambda bb, j: (bb, 0, 0)),
                  pl.BlockSpec((POOL_GROUPS, POOL_GROUP_DIM, POOL_GROUP_DIM), lambda bb, j: (0, 0, 0)),
                  pl.BlockSpec((1, D_MODEL), full),
                  pl.BlockSpec((1, D_MODEL), full),
                  pl.BlockSpec((1, D_MODEL), full),
                  pl.BlockSpec((D_MODEL, 2 * ROUTER_LANES), full),
                  pl.BlockSpec((D_MODEL, ROUTER_LANES), full),
                  pl.BlockSpec((1, ROUTER_LANES), full)],
        out_specs=[pl.BlockSpec((None, ts, D_MODEL), tile),
                   pl.BlockSpec((None, ts, D_MODEL), tile),
                   pl.BlockSpec((None, ts, ROUTER_LANES), tile),
                   pl.BlockSpec((None, hist, D_MODEL), lambda bb, j: (bb, 0, 0))],
        out_shape=[jax.ShapeDtypeStruct((bsz, s, D_MODEL), F32),
                   jax.ShapeDtypeStruct((bsz, s, D_MODEL), BF16),
                   jax.ShapeDtypeStruct((bsz, s, ROUTER_LANES), F32),
                   jax.ShapeDtypeStruct((bsz, hist, D_MODEL), F32)],
        scratch_shapes=[pltpu.VMEM((hist + ts, D_MODEL), F32)],
        compiler_params=_cparams("arbitrary", "arbitrary"),
        name="pool_mix_ln_route",
    )(x, p0, pw, sc, g, b, wr_a, wr_b, rb)


def _moe_kernel(xb_ref, gates_ref, wgu_ref, wd_ref, x1_ref, g_ref, b_ref, o_ref, acc):
    e = pl.program_id(1)

    @pl.when(e == 0)
    def _():
        acc[...] = jnp.zeros_like(acc)

    hu = jnp.dot(xb_ref[...], wgu_ref[...], preferred_element_type=F32)
    hg, hv = hu[:, :D_EXPERT], hu[:, D_EXPERT:]
    gates = gates_ref[...]
    lane = lax.broadcasted_iota(jnp.int32, gates.shape, 1)
    gate = jnp.sum(jnp.where(lane == e + EXPERT_LANE0, gates, 0.0), axis=1, keepdims=True)
    act = (jax.nn.silu(hg) * hv * gate).astype(BF16)
    acc[...] += jnp.dot(act, wd_ref[...], preferred_element_type=F32)

    @pl.when(e == N_EXPERTS - 1)
    def _():
        o_ref[...] = _layer_norm(DEEPNORM_ALPHA * x1_ref[...] + acc[...], g_ref[...], b_ref[...])


def _moe(xb, gates, wgu, wd, layer, x1, g, b, tm):
    t = xb.shape[0]
    tile = lambda i, e: (i, 0)
    full = lambda i, e: (0, 0)
    return pl.pallas_call(
        _moe_kernel,
        grid=(t // tm, N_EXPERTS),
        in_specs=[pl.BlockSpec((tm, D_MODEL), tile),
                  pl.BlockSpec((tm, ROUTER_LANES), tile),
                  pl.BlockSpec((None, None, D_MODEL, 2 * D_EXPERT), lambda i, e: (layer, e, 0, 0)),
                  pl.BlockSpec((None, None, D_EXPERT, D_MODEL), lambda i, e: (layer, e, 0, 0)),
                  pl.BlockSpec((tm, D_MODEL), tile),
                  pl.BlockSpec((1, D_MODEL), full),
                  pl.BlockSpec((1, D_MODEL), full)],
        out_specs=pl.BlockSpec((tm, D_MODEL), tile),
        out_shape=jax.ShapeDtypeStruct((t, D_MODEL), F32),
        scratch_shapes=[pltpu.VMEM((tm, D_MODEL), F32)],
        compiler_params=_cparams("arbitrary", "arbitrary"),
        name="moe_ffn_ln",
    )(xb, gates, wgu, wd, x1, g, b)


def _dispatch(route, tm):
    t = route.shape[0]
    n_tiles = t // tm + N_CLASSES
    pair_order = jnp.array(PAIR_ORDER, jnp.int32)
    cls = route[:, CLASS_LANE].astype(jnp.int32)
    pos = PAIRS_PER_GROUP * (cls // PAIRS_PER_GROUP) + pair_order[cls % PAIRS_PER_GROUP]
    cpos = jnp.arange(N_CLASSES, dtype=jnp.int32)
    exp_a = MOE_EXPERTS_PER_GROUP * (cpos // PAIRS_PER_GROUP) + jnp.array([p[0] for p in PAIR_SLOTS], jnp.int32)[cpos % PAIRS_PER_GROUP]
    exp_b = MOE_EXPERTS_PER_GROUP * (cpos // PAIRS_PER_GROUP) + jnp.array([p[1] for p in PAIR_SLOTS], jnp.int32)[cpos % PAIRS_PER_GROUP]
    onehot = (pos[:, None] == cpos[None, :]).astype(jnp.int32)
    csum = jnp.cumsum(onehot, axis=0)
    rank = jnp.sum(csum * onehot, axis=1) - 1
    counts = csum[-1]
    padded = ((counts + tm - 1) // tm) * tm
    ends = jnp.cumsum(padded)
    offs = ends - padded
    dest = offs[pos] + rank
    src = jnp.zeros((n_tiles * tm,), jnp.int32).at[dest].set(jnp.arange(t, dtype=jnp.int32))
    g_a = jnp.take_along_axis(route, (EXPERT_LANE0 + exp_a[pos])[:, None], axis=1)
    g_b = jnp.take_along_axis(route, (EXPERT_LANE0 + exp_b[pos])[:, None], axis=1)
    gab = jnp.zeros((n_tiles * tm, SUBLANES), F32).at[dest, 0:2].set(jnp.concatenate([g_a, g_b], axis=1))
    n_valid_tiles = ends[-1] // tm
    tile_start = jnp.arange(n_tiles, dtype=jnp.int32) * tm
    tile_pos = jnp.sum((tile_start[:, None] >= ends[None, :]).astype(jnp.int32), axis=1)
    live = jnp.arange(n_tiles) < n_valid_tiles
    tile_pos = jnp.where(live, tile_pos, tile_pos[n_valid_tiles - 1])
    tile_rows = jnp.where(live, jnp.clip(offs[tile_pos] + counts[tile_pos] - tile_start, 0, tm), 0)
    return (src.reshape(n_tiles, 1, tm), gab, exp_a[tile_pos], exp_b[tile_pos],
            tile_rows.astype(jnp.int32), n_valid_tiles.reshape(1).astype(jnp.int32))


def _moe_sparse_kernel(ea_ref, eb_ref, rows_ref, nvt_ref,
                       src_ref, srcn_ref, gab_ref, x1_hbm, wgu_a, wgu_b, wd_a, wd_b, g_ref, b_ref, spill_in,
                       x2_hbm, spill_hbm, xbuf, obuf, gsem, ssem, *, tm):
    del ea_ref, eb_ref, spill_in
    i = pl.program_id(0)
    nvt = nvt_ref[0]
    slot = i % 2

    def gather_copy(tok, r, s):
        return pltpu.make_async_copy(x1_hbm.at[pl.ds(tok, 1)], xbuf.at[s, pl.ds(r, 1)], gsem.at[s])

    def start_gather(idx_ref, s):
        def body(r, c):
            gather_copy(idx_ref[0, r], r, s).start()
            return c
        lax.fori_loop(0, tm, body, 0, unroll=8)

    def wait_tile(s, sem):
        pltpu.make_async_copy(x1_hbm.at[pl.ds(0, tm)], xbuf.at[s], sem.at[s]).wait()

    @pl.when(i == 0)
    def _():
        start_gather(src_ref, 0)

    @pl.when(i + 1 < nvt)
    def _():
        start_gather(srcn_ref, 1 - slot)

    @pl.when(i < nvt)
    def _():
        wait_tile(slot, gsem)

        @pl.when(i >= 2)
        def _():
            wait_tile(slot, ssem)

        xs = xbuf[slot]
        xb = xs.astype(BF16)
        gab = gab_ref[...]
        acc = None
        for col, (wgu, wd) in enumerate(((wgu_a, wd_a), (wgu_b, wd_b))):
            hu = jnp.dot(xb, wgu[...], preferred_element_type=F32)
            act = (jax.nn.silu(hu[:, :D_EXPERT]) * hu[:, D_EXPERT:] * gab[:, col:col + 1]).astype(BF16)
            out = jnp.dot(act, wd[...], preferred_element_type=F32)
            acc = out if acc is None else acc + out
        obuf[slot] = _layer_norm(DEEPNORM_ALPHA * xs + acc, g_ref[...], b_ref[...])

        rows = rows_ref[i]

        def put_row(r, c):
            pltpu.make_async_copy(obuf.at[slot, pl.ds(r, 1)], x2_hbm.at[pl.ds(src_ref[0, r], 1)], ssem.at[slot]).start()
            return c

        def put_pad(r, c):
            pltpu.make_async_copy(obuf.at[slot, pl.ds(r, 1)], spill_hbm.at[pl.ds(slot * tm + r, 1)], ssem.at[slot]).start()
            return c

        lax.fori_loop(0, rows, put_row, 0)
        lax.fori_loop(rows, tm, put_pad, 0)

    @pl.when(i == nvt - 1)
    def _():
        wait_tile(slot, ssem)

        @pl.when(i >= 1)
        def _():
            wait_tile(1 - slot, ssem)


def _moe_sparse(x1, route, wgu, wd, layer, g, b, tm):
    t = x1.shape[0]
    src, gab, tile_ea, tile_eb, tile_rows, nvt = _dispatch(route, tm)
    n_tiles = src.shape[0]
    full = lambda i, *_: (0, 0)
    grid_spec = pltpu.PrefetchScalarGridSpec(
        num_scalar_prefetch=4,
        grid=(n_tiles,),
        in_specs=[pl.BlockSpec((None, 1, tm), lambda i, *_: (i, 0, 0), memory_space=pltpu.SMEM),
                  pl.BlockSpec((None, 1, tm), lambda i, *_: (jnp.minimum(i + 1, n_tiles - 1), 0, 0),
                               memory_space=pltpu.SMEM),
                  pl.BlockSpec((tm, SUBLANES), lambda i, *_: (i, 0)),
                  pl.BlockSpec(memory_space=pl.ANY),
                  pl.BlockSpec((None, None, D_MODEL, 2 * D_EXPERT), lambda i, ea, eb, *_: (layer, ea[i], 0, 0)),
                  pl.BlockSpec((None, None, D_MODEL, 2 * D_EXPERT), lambda i, ea, eb, *_: (layer, eb[i], 0, 0)),
                  pl.BlockSpec((None, None, D_EXPERT, D_MODEL), lambda i, ea, eb, *_: (layer, ea[i], 0, 0)),
                  pl.BlockSpec((None, None, D_EXPERT, D_MODEL), lambda i, ea, eb, *_: (layer, eb[i], 0, 0)),
                  pl.BlockSpec((1, D_MODEL), full),
                  pl.BlockSpec((1, D_MODEL), full),
                  pl.BlockSpec(memory_space=pl.ANY)],
        out_specs=[pl.BlockSpec(memory_space=pl.ANY), pl.BlockSpec(memory_space=pl.ANY)],
        scratch_shapes=[pltpu.VMEM((2, tm, D_MODEL), F32), pltpu.VMEM((2, tm, D_MODEL), F32),
                        pltpu.SemaphoreType.DMA((2,)), pltpu.SemaphoreType.DMA((2,))],
    )
    x2, _ = pl.pallas_call(
        functools.partial(_moe_sparse_kernel, tm=tm),
        grid_spec=grid_spec,
        out_shape=[jax.ShapeDtypeStruct((t, D_MODEL), F32), jax.ShapeDtypeStruct((2 * tm, D_MODEL), F32)],
        input_output_aliases={14: 1},
        compiler_params=_cparams("arbitrary"),
        name="moe_sparse_ln",
    )(tile_ea, tile_eb, tile_rows, nvt, src, src, gab, x1, wgu, wgu, wd, wd, g, b,
      jnp.zeros((2 * tm, D_MODEL), F32))
    return x2


def _router_weights(wg, bg, we, be):
    w = jnp.concatenate([wg, we], axis=1)
    w = jnp.pad(w, ((0, 0), (0, ROUTER_LANES - w.shape[1])))
    w_hi = w.astype(BF16)
    w_lo = (w - w_hi.astype(F32)).astype(BF16)
    rb = jnp.pad(jnp.concatenate([bg, be]), (0, ROUTER_LANES - MOE_GROUPS - N_EXPERTS))
    return jnp.concatenate([w_hi, w_lo], axis=1), w_hi, rb.reshape(1, ROUTER_LANES)


def _trunk(x, lru_h0, lru_conv0, pool0, pos0, wts, ts, tm, sparse):
    bsz, s, d = x.shape
    t = bsz * s
    row = lambda v: v.reshape(1, -1)
    proj = _proj(x.reshape(t, d), wts["w_in"], min(512, t), 1024).reshape(bsz, s, 2 * D_RNN)
    y, h_last, conv_new = _lru_scan(
        proj, lru_h0.reshape(bsz, 1, D_RNN), lru_conv0, wts["conv_w"], row(wts["conv_b"]),
        wts["w_a"], row(wts["b_a"]), wts["w_x"], row(wts["b_x"]), row(wts["lam"]), ts)
    x1, x1b, gates = _out_proj(y.reshape(t, D_RNN), wts["w_out"], x.reshape(t, d),
                               row(wts["ln_g"][0, 0]), row(wts["ln_b"][0, 0]), *wts["router"][0], tm)
    def moe(layer, x1, x1b, gates):
        g, b = row(wts["ln_g"][layer, 1]), row(wts["ln_b"][layer, 1])
        if sparse:
            return _moe_sparse(x1, gates, wts["wgu"], wts["wd"], layer, g, b, MOE_TILE)
        return _moe(x1b, gates, wts["wgu"], wts["wd"], layer, x1, g, b, min(512, t))

    x2 = moe(0, x1, x1b, gates)
    p0 = jnp.pad(pool0, ((0, 0), (1, 0), (0, 0)))
    x1, x1b, gates, pool_new = _pool_mix(
        x2.reshape(bsz, s, d), p0, wts["pool_w"], row(wts["pool_scale"]),
        row(wts["ln_g"][1, 0]), row(wts["ln_b"][1, 0]), *wts["router"][1], ts, pos0)
    y_out = moe(1, x1.reshape(t, d), x1b.reshape(t, d), gates.reshape(t, ROUTER_LANES))
    return (y_out.reshape(bsz, s, d), h_last.reshape(1, bsz, D_RNN), conv_new[None],
            pool_new[None, :, 1:, :])


def kernel(x_prompt, x_sample, state_lru_h, state_lru_conv, state_pool, lru_w_in, lru_conv_w, lru_conv_b, lru_w_a, lru_b_a, lru_w_x, lru_b_x, lru_lambda, lru_w_out, pool_w, pool_scale, ln_g, ln_b, moe_router_g_w, moe_router_g_b, moe_router_e_w, moe_router_e_b, moe_w_gate_up, moe_w_down):
    wts = {
        "w_in": lru_w_in[0].astype(BF16), "conv_w": lru_conv_w[0], "conv_b": lru_conv_b[0],
        "w_a": lru_w_a[0].astype(BF16), "b_a": lru_b_a[0], "w_x": lru_w_x[0].astype(BF16), "b_x": lru_b_x[0],
        "lam": lru_lambda[0], "w_out": lru_w_out[0].astype(BF16),
        "pool_w": pool_w[0].astype(BF16), "pool_scale": pool_scale[0], "ln_g": ln_g, "ln_b": ln_b,
        "router": [_router_weights(moe_router_g_w[i], moe_router_g_b[i], moe_router_e_w[i], moe_router_e_b[i])
                   for i in range(DEPTH)],
        "wgu": moe_w_gate_up.astype(BF16), "wd": moe_w_down.astype(BF16),
    }
    bp = x_prompt.shape[0]
    dt = x_prompt.dtype
    y_p, h_p, c_p, p_p = _trunk(
        x_prompt, jnp.zeros((bp, D_RNN), dt), jnp.zeros((bp, CONV_WIDTH - 1, D_RNN), dt),
        jnp.zeros((bp, POOL_STATE, D_MODEL), dt), 0, wts, ts=256, tm=256, sparse=True)
    y_s, h_s, c_s, p_s = _trunk(
        x_sample, state_lru_h[0], state_lru_conv[0], state_pool[0], PAST_LEN, wts,
        ts=x_sample.shape[1], tm=256, sparse=False)
    return (y_p, y_s, h_p, c_p, p_p, h_s, c_s, p_s)
```

```python
import functools

import jax
import jax.numpy as jnp
from jax import lax
from jax.experimental import pallas as pl
from jax.experimental.pallas import tpu as pltpu

F32 = jnp.float32
BF16 = jnp.bfloat16

D_MODEL = 2048
DEPTH = 2
D_RNN = D_MODEL
LRU_HEADS = 8
LRU_BLOCK = D_RNN // LRU_HEADS
CONV_WIDTH = 4
LRU_C = 8.0
POOL_WINDOWS = (2, 4, 8, 16)
POOL_GROUPS = 4
POOL_GROUP_DIM = D_MODEL // POOL_GROUPS
POOL_STATE = max(POOL_WINDOWS) - 1
MOE_GROUPS = 4
MOE_EXPERTS_PER_GROUP = 4
N_EXPERTS = MOE_GROUPS * MOE_EXPERTS_PER_GROUP
D_EXPERT = 512
DEEPNORM_ALPHA = (2.0 * DEPTH) ** 0.25
LN_EPS = 1e-5
PAST_LEN = 2048

LANES = 128
SUBLANES = 8
ROUTER_LANES = LANES
EXPERT_LANE0 = MOE_GROUPS
CLASS_LANE = 0
RANK_LANE = 1
GATE_LANE0 = 2
PAIRS_PER_GROUP = 6
N_CLASSES = MOE_GROUPS * PAIRS_PER_GROUP
PAIR_SLOTS = ((0, 1), (0, 2), (0, 3), (1, 3), (1, 2), (3, 2))
PAIR_ORDER = (0, 1, 2, 4, 3, 5)
VMEM_LIMIT = 56 * 1024 * 1024
MOE_TILE = 256


def _cparams(*sem):
    return pltpu.CompilerParams(dimension_semantics=sem, vmem_limit_bytes=VMEM_LIMIT)


def _layer_norm(z, g, b):
    mu = jnp.mean(z, axis=-1, keepdims=True)
    zc = z - mu
    var = jnp.mean(zc * zc, axis=-1, keepdims=True)
    return zc * lax.rsqrt(var + LN_EPS) * g + b


def _route(x1, wr_a_ref, wr_b_ref, rb_ref, cnt_ref, first):
    @pl.when(first)
    def _():
        cnt_ref[...] = jnp.zeros_like(cnt_ref)

    hi = x1.astype(BF16)
    lo = (x1 - hi.astype(F32)).astype(BF16)
    c = jnp.dot(hi, wr_a_ref[...], preferred_element_type=F32)
    d = jnp.dot(lo, wr_b_ref[...], preferred_element_type=F32)
    logits = c[:, :ROUTER_LANES] + c[:, ROUTER_LANES:] + d + rb_ref[...]
    tm = logits.shape[0]
    lane = lax.broadcasted_iota(jnp.int32, (tm, ROUTER_LANES), 1)
    neg = jnp.float32(-jnp.inf)
    is_g = lane < MOE_GROUPS
    lg = jnp.where(is_g, logits, neg)
    gmax = jnp.max(lg, axis=1, keepdims=True)
    gsel = jnp.min(jnp.where(lg == gmax, lane, ROUTER_LANES), axis=1, keepdims=True)
    p_sel = 1.0 / jnp.sum(jnp.where(is_g, jnp.exp(logits - gmax), 0.0), axis=1, keepdims=True)
    elane = lane - EXPERT_LANE0
    in_g = (elane >= 0) & (elane < N_EXPERTS) & ((elane >> 2) == gsel)
    le = jnp.where(in_g, logits, neg)
    t1 = jnp.max(le, axis=1, keepdims=True)
    i1 = jnp.min(jnp.where(le == t1, lane, ROUTER_LANES), axis=1, keepdims=True)
    le2 = jnp.where(lane == i1, neg, le)
    t2 = jnp.max(le2, axis=1, keepdims=True)
    i2 = jnp.min(jnp.where(le2 == t2, lane, ROUTER_LANES), axis=1, keepdims=True)
    e2 = jnp.exp(t2 - t1)
    inv = p_sel / (1.0 + e2)
    gates = jnp.where(lane == i1, inv, 0.0) + jnp.where(lane == i2, inv * e2, 0.0)
    base = EXPERT_LANE0 + MOE_EXPERTS_PER_GROUP * gsel
    pa = jnp.minimum(i1, i2) - base
    pb = jnp.maximum(i1, i2) - base
    cls = PAIRS_PER_GROUP * gsel + 3 * pa - ((pa * (pa - 1)) >> 1) + pb - pa - 1
    kp = cls - PAIRS_PER_GROUP * gsel
    kp = jnp.where(kp == 3, 4, jnp.where(kp == 4, 3, kp))
    pos = PAIRS_PER_GROUP * gsel + kp
    first_local = jnp.where(kp < 3, 0, jnp.where(kp < 5, 1, 3))
    i1_first = (i1 - base) == first_local
    g_first = jnp.where(i1_first, inv, inv * e2)
    g_second = jnp.where(i1_first, inv * e2, inv)
    onehot = lane == pos
    tri = (lax.broadcasted_iota(jnp.int32, (tm, tm), 0) >= lax.broadcasted_iota(jnp.int32, (tm, tm), 1))
    cum = jnp.dot(tri.astype(BF16), onehot.astype(BF16), preferred_element_type=F32)
    run = cnt_ref[0:1, :]
    rank = jnp.sum(jnp.where(onehot, cum + run, 0.0), axis=1, keepdims=True) - 1.0
    cnt_ref[...] = jnp.broadcast_to(run + cum[tm - 1:tm, :], cnt_ref.shape)
    for ln, val in ((CLASS_LANE, pos.astype(F32)), (RANK_LANE, rank), (GATE_LANE0, g_first), (GATE_LANE0 + 1, g_second)):
        gates = jnp.where(lane == ln, val, gates)
    return hi, gates


def _proj_kernel(x_ref, w_ref, o_ref):
    o_ref[...] = jnp.dot(x_ref[...].astype(BF16), w_ref[...], preferred_element_type=F32)


def _proj(x, w, tm, tn):
    t, k = x.shape
    n = w.shape[1]
    return pl.pallas_call(
        _proj_kernel,
        grid=(n // tn, t // tm),
        in_specs=[pl.BlockSpec((tm, k), lambda jn, im: (im, 0)),
                  pl.BlockSpec((k, tn), lambda jn, im: (0, jn))],
        out_specs=pl.BlockSpec((tm, tn), lambda jn, im: (im, jn)),
        out_shape=jax.ShapeDtypeStruct((t, n), F32),
        compiler_params=_cparams("arbitrary", "arbitrary"),
        name="lru_in_proj",
    )(x, w)


def _lru_kernel(gate_ref, u_ref, h0_ref, conv0_ref, cw_ref, cb_ref, wa_ref, ba_ref, wx_ref, bx_ref,
                lam_ref, y_ref, hl_ref, cn_ref, ubuf, hc, *, ts):
    j = pl.program_id(1)

    @pl.when(j == 0)
    def _():
        ubuf[0:SUBLANES, :] = jnp.zeros((SUBLANES, D_RNN), F32)
        ubuf[SUBLANES - (CONV_WIDTH - 1):SUBLANES, :] = conv0_ref[...]
        hc[...] = h0_ref[...]

    u = u_ref[...]
    ubuf[SUBLANES:SUBLANES + ts, :] = u
    uc = cb_ref[...] + ubuf[SUBLANES - 3:SUBLANES - 3 + ts, :] * cw_ref[0:1, :]
    uc = uc + ubuf[SUBLANES - 2:SUBLANES - 2 + ts, :] * cw_ref[1:2, :]
    uc = uc + ubuf[SUBLANES - 1:SUBLANES - 1 + ts, :] * cw_ref[2:3, :]
    uc = uc + u * cw_ref[3:4, :]
    cn_ref[...] = ubuf[SUBLANES + ts - (CONV_WIDTH - 1):SUBLANES + ts, :]
    ubuf[0:SUBLANES, :] = ubuf[ts:ts + SUBLANES, :]

    ucb = uc.astype(BF16)
    rs, igs = [], []
    for h in range(LRU_HEADS):
        blk = ucb[:, h * LRU_BLOCK:(h + 1) * LRU_BLOCK]
        rs.append(jnp.dot(blk, wa_ref[h], preferred_element_type=F32))
        igs.append(jnp.dot(blk, wx_ref[h], preferred_element_type=F32))
    r = jax.nn.sigmoid(jnp.concatenate(rs, axis=1) + ba_ref[...])
    ig = jax.nn.sigmoid(jnp.concatenate(igs, axis=1) + bx_ref[...])
    log_a = (-LRU_C) * r * jax.nn.softplus(-lam_ref[...])
    a = jnp.exp(log_a)
    b = jnp.sqrt(-jnp.tanh(log_a) * (a * a + 1.0)) * ig * uc

    row = lax.broadcasted_iota(jnp.int32, (ts, D_RNN), 0) & (SUBLANES - 1)
    for s in (1, 2, 4):
        keep = row >= s
        a_s = pltpu.roll(a, s, 0)
        b_s = pltpu.roll(b, s, 0)
        b = jnp.where(keep, a * b_s + b, b)
        a = jnp.where(keep, a * a_s, a)
    hprev = hc[...]
    hs = []
    for g in range(ts // SUBLANES):
        hg = a[g * SUBLANES:(g + 1) * SUBLANES] * hprev + b[g * SUBLANES:(g + 1) * SUBLANES]
        hs.append(hg)
        hprev = hg[SUBLANES - 1:SUBLANES]
    h = jnp.concatenate(hs, axis=0)
    hc[...] = hprev
    hl_ref[...] = hprev
    y_ref[...] = (h * jax.nn.gelu(gate_ref[...])).astype(BF16)


def _lru_scan(proj, h0, conv0, cw, cb, wa, ba, wx, bx, lam, ts):
    bsz, s, _ = proj.shape
    row2 = lambda b, j: (0, 0)
    return pl.pallas_call(
        functools.partial(_lru_kernel, ts=ts),
        grid=(bsz, s // ts),
        in_specs=[pl.BlockSpec((None, ts, D_RNN), lambda b, j: (b, j, 0)),
                  pl.BlockSpec((None, ts, D_RNN), lambda b, j: (b, j, 1)),
                  pl.BlockSpec((None, 1, D_RNN), lambda b, j: (b, 0, 0)),
                  pl.BlockSpec((None, CONV_WIDTH - 1, D_RNN), lambda b, j: (b, 0, 0)),
                  pl.BlockSpec((CONV_WIDTH, D_RNN), row2),
                  pl.BlockSpec((1, D_RNN), row2),
                  pl.BlockSpec((LRU_HEADS, LRU_BLOCK, LRU_BLOCK), lambda b, j: (0, 0, 0)),
                  pl.BlockSpec((1, D_RNN), row2),
                  pl.BlockSpec((LRU_HEADS, LRU_BLOCK, LRU_BLOCK), lambda b, j: (0, 0, 0)),
                  pl.BlockSpec((1, D_RNN), row2),
                  pl.BlockSpec((1, D_RNN), row2)],
        out_specs=[pl.BlockSpec((None, ts, D_RNN), lambda b, j: (b, j, 0)),
                   pl.BlockSpec((None, 1, D_RNN), lambda b, j: (b, 0, 0)),
                   pl.BlockSpec((None, CONV_WIDTH - 1, D_RNN), lambda b, j: (b, 0, 0))],
        out_shape=[jax.ShapeDtypeStruct((bsz, s, D_RNN), BF16),
                   jax.ShapeDtypeStruct((bsz, 1, D_RNN), F32),
                   jax.ShapeDtypeStruct((bsz, CONV_WIDTH - 1, D_RNN), F32)],
        scratch_shapes=[pltpu.VMEM((SUBLANES + ts, D_RNN), F32), pltpu.VMEM((1, D_RNN), F32)],
        compiler_params=_cparams("arbitrary", "arbitrary"),
        name="lru_scan",
    )(proj, proj, h0, conv0, cw, cb, wa, ba, wx, bx, lam)


def _out_kernel(y_ref, w_ref, x_ref, g_ref, b_ref, wr_a_ref, wr_b_ref, rb_ref,
                x1_ref, x1b_ref, gates_ref, cnt_ref):
    mix = jnp.dot(y_ref[...], w_ref[...], preferred_element_type=F32)
    x1 = _layer_norm(DEEPNORM_ALPHA * x_ref[...] + mix, g_ref[...], b_ref[...])
    x1_ref[...] = x1
    x1b_ref[...], gates_ref[...] = _route(x1, wr_a_ref, wr_b_ref, rb_ref, cnt_ref, pl.program_id(0) == 0)


def _out_proj(y, w, x, g, b, wr_a, wr_b, rb, tm):
    t = x.shape[0]
    full = lambda i: (0, 0)
    tile = lambda i: (i, 0)
    return pl.pallas_call(
        _out_kernel,
        grid=(t // tm,),
        in_specs=[pl.BlockSpec((tm, D_RNN), tile),
                  pl.BlockSpec((D_RNN, D_MODEL), full),
                  pl.BlockSpec((tm, D_MODEL), tile),
                  pl.BlockSpec((1, D_MODEL), full),
                  pl.BlockSpec((1, D_MODEL), full),
                  pl.BlockSpec((D_MODEL, 2 * ROUTER_LANES), full),
                  pl.BlockSpec((D_MODEL, ROUTER_LANES), full),
                  pl.BlockSpec((1, ROUTER_LANES), full)],
        out_specs=[pl.BlockSpec((tm, D_MODEL), tile),
                   pl.BlockSpec((tm, D_MODEL), tile),
                   pl.BlockSpec((tm, ROUTER_LANES), tile),
                   pl.BlockSpec((SUBLANES, ROUTER_LANES), full)],
        out_shape=[jax.ShapeDtypeStruct((t, D_MODEL), F32),
                   jax.ShapeDtypeStruct((t, D_MODEL), BF16),
                   jax.ShapeDtypeStruct((t, ROUTER_LANES), F32),
                   jax.ShapeDtypeStruct((SUBLANES, ROUTER_LANES), F32)],
        compiler_params=_cparams("arbitrary"),
        name="lru_out_ln_route",
    )(y, w, x, g, b, wr_a, wr_b, rb)


def _pool_kernel(x_ref, p0_ref, pw_ref, sc_ref, g_ref, b_ref, wr_a_ref, wr_b_ref, rb_ref,
                 x1_ref, x1b_ref, gates_ref, pn_ref, cnt_ref, ext, *, ts, pos0):
    j = pl.program_id(1)
    hist = POOL_STATE + 1

    @pl.when(j == 0)
    def _():
        ext[0:hist, :] = p0_ref[...]

    x = x_ref[...]
    ext[hist:hist + ts, :] = x
    pos = (pos0 + j * ts + lax.broadcasted_iota(jnp.int32, (ts, 1), 0)).astype(F32)
    outs = []
    for g, w in enumerate(POOL_WINDOWS):
        lo, hi = g * POOL_GROUP_DIM, (g + 1) * POOL_GROUP_DIM
        acc = x[:, lo:hi]
        for k in range(1, w):
            acc = acc + ext[hist - k:hist - k + ts, lo:hi]
        mean = acc / jnp.minimum(jnp.float32(w), pos + 1.0)
        diff = (mean - x[:, lo:hi]).astype(BF16)
        outs.append(jnp.dot(diff, pw_ref[g], preferred_element_type=F32))
    mix = jnp.concatenate(outs, axis=1) * sc_ref[...]
    pn_ref[...] = ext[ts:ts + hist, :]
    ext[0:hist, :] = ext[ts:ts + hist, :]
    x1 = _layer_norm(DEEPNORM_ALPHA * x + mix, g_ref[...], b_ref[...])
    x1_ref[...] = x1
    first = (pl.program_id(0) == 0) & (j == 0)
    x1b_ref[...], gates_ref[...] = _route(x1, wr_a_ref, wr_b_ref, rb_ref, cnt_ref, first)


def _pool_mix(x, p0, pw, sc, g, b, wr_a, wr_b, rb, ts, pos0):
    bsz, s, _ = x.shape
    hist = POOL_STATE + 1
    full = lambda bb, j: (0, 0)
    tile = lambda bb, j: (bb, j, 0)
    return pl.pallas_call(
        functools.partial(_pool_kernel, ts=ts, pos0=pos0),
        grid=(bsz, s // ts),
        in_specs=[pl.BlockSpec((None, ts, D_MODEL), tile),
                  pl.BlockSpec((None, hist, D_MODEL), lambda bb, j: (bb, 0, 0)),
                  pl.BlockSpec((POOL_GROUPS, POOL_GROUP_DIM, POOL_GROUP_DIM), lambda bb, j: (0, 0, 0)),
                  pl.BlockSpec((1, D_MODEL), full),
                  pl.BlockSpec((1, D_MODEL), full),
                  pl.BlockSpec((1, D_MODEL), full),
                  pl.BlockSpec((D_MODEL, 2 * ROUTER_LANES), full),
                  pl.BlockSpec((D_MODEL, ROUTER_LANES), full),
                  pl.BlockSpec((1, ROUTER_LANES), full)],
        out_specs=[pl.BlockSpec((None, ts, D_MODEL), tile),
                   pl.BlockSpec((None, ts, D_MODEL), tile),
                   pl.BlockSpec((None, ts, ROUTER_LANES), tile),
                   pl.BlockSpec((None, hist, D_MODEL), lambda bb, j: (bb, 0, 0)),
                   pl.BlockSpec((SUBLANES, ROUTER_LANES), full)],
        out_shape=[jax.ShapeDtypeStruct((bsz, s, D_MODEL), F32),
                   jax.ShapeDtypeStruct((bsz, s, D_MODEL), BF16),
                   jax.ShapeDtypeStruct((bsz, s, ROUTER_LANES), F32),
                   jax.ShapeDtypeStruct((bsz, hist, D_MODEL), F32),
                   jax.ShapeDtypeStruct((SUBLANES, ROUTER_LANES), F32)],
        scratch_shapes=[pltpu.VMEM((hist + ts, D_MODEL), F32)],
        compiler_params=_cparams("arbitrary", "arbitrary"),
        name="pool_mix_ln_route",
    )(x, p0, pw, sc, g, b, wr_a, wr_b, rb)


def _moe_kernel(xb_ref, gates_ref, wgu_ref, wd_ref, x1_ref, g_ref, b_ref, o_ref, acc):
    e = pl.program_id(1)

    @pl.when(e == 0)
    def _():
        acc[...] = jnp.zeros_like(acc)

    hu = jnp.dot(xb_ref[...], wgu_ref[...], preferred_element_type=F32)
    hg, hv = hu[:, :D_EXPERT], hu[:, D_EXPERT:]
    gates = gates_ref[...]
    lane = lax.broadcasted_iota(jnp.int32, gates.shape, 1)
    gate = jnp.sum(jnp.where(lane == e + EXPERT_LANE0, gates, 0.0), axis=1, keepdims=True)
    act = (jax.nn.silu(hg) * hv * gate).astype(BF16)
    acc[...] += jnp.dot(act, wd_ref[...], preferred_element_type=F32)

    @pl.when(e == N_EXPERTS - 1)
    def _():
        o_ref[...] = _layer_norm(DEEPNORM_ALPHA * x1_ref[...] + acc[...], g_ref[...], b_ref[...])


def _moe(xb, gates, wgu, wd, layer, x1, g, b, tm):
    t = xb.shape[0]
    tile = lambda i, e: (i, 0)
    full = lambda i, e: (0, 0)
    return pl.pallas_call(
        _moe_kernel,
        grid=(t // tm, N_EXPERTS),
        in_specs=[pl.BlockSpec((tm, D_MODEL), tile),
                  pl.BlockSpec((tm, ROUTER_LANES), tile),
                  pl.BlockSpec((None, None, D_MODEL, 2 * D_EXPERT), lambda i, e: (layer, e, 0, 0)),
                  pl.BlockSpec((None, None, D_EXPERT, D_MODEL), lambda i, e: (layer, e, 0, 0)),
                  pl.BlockSpec((tm, D_MODEL), tile),
                  pl.BlockSpec((1, D_MODEL), full),
                  pl.BlockSpec((1, D_MODEL), full)],
        out_specs=pl.BlockSpec((tm, D_MODEL), tile),
        out_shape=jax.ShapeDtypeStruct((t, D_MODEL), F32),
        scratch_shapes=[pltpu.VMEM((tm, D_MODEL), F32)],
        compiler_params=_cparams("arbitrary", "arbitrary"),
        name="moe_ffn_ln",
    )(xb, gates, wgu, wd, x1, g, b)


def _dispatch(route, counts, tm):
    t = route.shape[0]
    n_tiles = t // tm + N_CLASSES
    cpos = jnp.arange(N_CLASSES, dtype=jnp.int32)
    slots = jnp.array(PAIR_SLOTS, jnp.int32)
    exp_ab = MOE_EXPERTS_PER_GROUP * (cpos // PAIRS_PER_GROUP)[:, None] + slots[cpos % PAIRS_PER_GROUP]
    pos = route[:, CLASS_LANE].astype(jnp.int32)
    rank = route[:, RANK_LANE].astype(jnp.int32)
    counts = counts[0, :N_CLASSES].astype(jnp.int32)
    padded = ((counts + tm - 1) // tm) * tm
    ends = jnp.cumsum(padded)
    offs = ends - padded
    dest = offs[pos] + rank
    src = jnp.zeros((n_tiles * tm,), jnp.int32).at[dest].set(jnp.arange(t, dtype=jnp.int32))
    n_valid_tiles = ends[-1] // tm
    tile_start = jnp.arange(n_tiles, dtype=jnp.int32) * tm
    tile_pos = jnp.sum((tile_start[:, None] >= ends[None, :]).astype(jnp.int32), axis=1)
    live = jnp.arange(n_tiles) < n_valid_tiles
    tile_pos = jnp.where(live, tile_pos, tile_pos[n_valid_tiles - 1])
    tile_rows = jnp.where(live, jnp.clip(offs[tile_pos] + counts[tile_pos] - tile_start, 0, tm), 0)
    real = (jnp.arange(tm, dtype=jnp.int32)[None, :] < tile_rows[:, None]).reshape(n_tiles * tm, 1)
    gab = jnp.where(real, route[src, GATE_LANE0:GATE_LANE0 + SUBLANES], 0.0)
    return (src.reshape(n_tiles, 1, tm), gab, exp_ab[tile_pos, 0], exp_ab[tile_pos, 1],
            tile_rows.astype(jnp.int32), n_valid_tiles.reshape(1).astype(jnp.int32))


def _moe_sparse_kernel(ea_ref, eb_ref, rows_ref, nvt_ref,
                       src_ref, srcn_ref, gab_ref, x1_hbm, wgu_a, wgu_b, wd_a, wd_b, g_ref, b_ref, spill_in,
                       x2_hbm, spill_hbm, xbuf, obuf, gsem, ssem, *, tm):
    del ea_ref, eb_ref, spill_in
    i = pl.program_id(0)
    nvt = nvt_ref[0]
    slot = i % 2

    def gather_copy(tok, r, s):
        return pltpu.make_async_copy(x1_hbm.at[pl.ds(tok, 1)], xbuf.at[s, pl.ds(r, 1)], gsem.at[s])

    def start_gather(idx_ref, s):
        def body(r, c):
            gather_copy(idx_ref[0, r], r, s).start()
            return c
        lax.fori_loop(0, tm, body, 0, unroll=8)

    def wait_tile(s, sem):
        pltpu.make_async_copy(x1_hbm.at[pl.ds(0, tm)], xbuf.at[s], sem.at[s]).wait()

    @pl.when(i == 0)
    def _():
        start_gather(src_ref, 0)

    @pl.when(i + 1 < nvt)
    def _():
        start_gather(srcn_ref, 1 - slot)

    @pl.when(i < nvt)
    def _():
        wait_tile(slot, gsem)

        @pl.when(i >= 2)
        def _():
            wait_tile(slot, ssem)

        xs = xbuf[slot]
        xb = xs.astype(BF16)
        gab = gab_ref[...]
        acc = None
        for col, (wgu, wd) in enumerate(((wgu_a, wd_a), (wgu_b, wd_b))):
            hu = jnp.dot(xb, wgu[...], preferred_element_type=F32)
            act = (jax.nn.silu(hu[:, :D_EXPERT]) * hu[:, D_EXPERT:] * gab[:, col:col + 1]).astype(BF16)
            out = jnp.dot(act, wd[...], preferred_element_type=F32)
            acc = out if acc is None else acc + out
        obuf[slot] = _layer_norm(DEEPNORM_ALPHA * xs + acc, g_ref[...], b_ref[...])

        rows = rows_ref[i]

        def put_row(r, c):
            pltpu.make_async_copy(obuf.at[slot, pl.ds(r, 1)], x2_hbm.at[pl.ds(src_ref[0, r], 1)], ssem.at[slot]).start()
            return c

        def put_pad(r, c):
            pltpu.make_async_copy(obuf.at[slot, pl.ds(r, 1)], spill_hbm.at[pl.ds(slot * tm + r, 1)], ssem.at[slot]).start()
            return c

        lax.fori_loop(0, rows, put_row, 0)
        lax.fori_loop(rows, tm, put_pad, 0)

    @pl.when(i == nvt - 1)
    def _():
        wait_tile(slot, ssem)

        @pl.when(i >= 1)
        def _():
            wait_tile(1 - slot, ssem)


def _moe_sparse(x1, route, counts, wgu, wd, layer, g, b, tm):
    t = x1.shape[0]
    src, gab, tile_ea, tile_eb, tile_rows, nvt = _dispatch(route, counts, tm)
    n_tiles = src.shape[0]
    full = lambda i, *_: (0, 0)
    grid_spec = pltpu.PrefetchScalarGridSpec(
        num_scalar_prefetch=4,
        grid=(n_tiles,),
        in_specs=[pl.BlockSpec((None, 1, tm), lambda i, *_: (i, 0, 0), memory_space=pltpu.SMEM),
                  pl.BlockSpec((None, 1, tm), lambda i, *_: (jnp.minimum(i + 1, n_tiles - 1), 0, 0),
                               memory_space=pltpu.SMEM),
                  pl.BlockSpec((tm, SUBLANES), lambda i, *_: (i, 0)),
                  pl.BlockSpec(memory_space=pl.ANY),
                  pl.BlockSpec((None, None, D_MODEL, 2 * D_EXPERT), lambda i, ea, eb, *_: (layer, ea[i], 0, 0)),
                  pl.BlockSpec((None, None, D_MODEL, 2 * D_EXPERT), lambda i, ea, eb, *_: (layer, eb[i], 0, 0)),
                  pl.BlockSpec((None, None, D_EXPERT, D_MODEL), lambda i, ea, eb, *_: (layer, ea[i], 0, 0)),
                  pl.BlockSpec((None, None, D_EXPERT, D_MODEL), lambda i, ea, eb, *_: (layer, eb[i], 0, 0)),
                  pl.BlockSpec((1, D_MODEL), full),
                  pl.BlockSpec((1, D_MODEL), full),
                  pl.BlockSpec(memory_space=pl.ANY)],
        out_specs=[pl.BlockSpec(memory_space=pl.ANY), pl.BlockSpec(memory_space=pl.ANY)],
        scratch_shapes=[pltpu.VMEM((2, tm, D_MODEL), F32), pltpu.VMEM((2, tm, D_MODEL), F32),
                        pltpu.SemaphoreType.DMA((2,)), pltpu.SemaphoreType.DMA((2,))],
    )
    x2, _ = pl.pallas_call(
        functools.partial(_moe_sparse_kernel, tm=tm),
        grid_spec=grid_spec,
        out_shape=[jax.ShapeDtypeStruct((t, D_MODEL), F32), jax.ShapeDtypeStruct((2 * tm, D_MODEL), F32)],
        input_output_aliases={14: 1},
        compiler_params=_cparams("arbitrary"),
        name="moe_sparse_ln",
    )(tile_ea, tile_eb, tile_rows, nvt, src, src, gab, x1, wgu, wgu, wd, wd, g, b,
      jnp.zeros((2 * tm, D_MODEL), F32))
    return x2


def _router_weights(wg, bg, we, be):
    w = jnp.concatenate([wg, we], axis=1)
    w = jnp.pad(w, ((0, 0), (0, ROUTER_LANES - w.shape[1])))
    w_hi = w.astype(BF16)
    w_lo = (w - w_hi.astype(F32)).astype(BF16)
    rb = jnp.pad(jnp.concatenate([bg, be]), (0, ROUTER_LANES - MOE_GROUPS - N_EXPERTS))
    return jnp.concatenate([w_hi, w_lo], axis=1), w_hi, rb.reshape(1, ROUTER_LANES)


def _trunk(x, lru_h0, lru_conv0, pool0, pos0, wts, ts, tm, sparse):
    bsz, s, d = x.shape
    t = bsz * s
    row = lambda v: v.reshape(1, -1)
    proj = _proj(x.reshape(t, d), wts["w_in"], min(512, t), 1024).reshape(bsz, s, 2 * D_RNN)
    y, h_last, conv_new = _lru_scan(
        proj, lru_h0.reshape(bsz, 1, D_RNN), lru_conv0, wts["conv_w"], row(wts["conv_b"]),
        wts["w_a"], row(wts["b_a"]), wts["w_x"], row(wts["b_x"]), row(wts["lam"]), ts)
    x1, x1b, gates, counts = _out_proj(y.reshape(t, D_RNN), wts["w_out"], x.reshape(t, d),
                               row(wts["ln_g"][0, 0]), row(wts["ln_b"][0, 0]), *wts["router"][0], tm)
    def moe(layer, x1, x1b, gates, counts):
        g, b = row(wts["ln_g"][layer, 1]), row(wts["ln_b"][layer, 1])
        if sparse:
            return _moe_sparse(x1, gates, counts, wts["wgu"], wts["wd"], layer, g, b, MOE_TILE)
        return _moe(x1b, gates, wts["wgu"], wts["wd"], layer, x1, g, b, min(512, t))

    x2 = moe(0, x1, x1b, gates, counts)
    p0 = jnp.pad(pool0, ((0, 0), (1, 0), (0, 0)))
    x1, x1b, gates, pool_new, counts = _pool_mix(
        x2.reshape(bsz, s, d), p0, wts["pool_w"], row(wts["pool_scale"]),
        row(wts["ln_g"][1, 0]), row(wts["ln_b"][1, 0]), *wts["router"][1], ts, pos0)
    y_out = moe(1, x1.reshape(t, d), x1b.reshape(t, d), gates.reshape(t, ROUTER_LANES), counts)
    return (y_out.reshape(bsz, s, d), h_last.reshape(1, bsz, D_RNN), conv_new[None],
            pool_new[None, :, 1:, :])


def kernel(x_prompt, x_sample, state_lru_h, state_lru_conv, state_pool, lru_w_in, lru_conv_w, lru_conv_b, lru_w_a, lru_b_a, lru_w_x, lru_b_x, lru_lambda, lru_w_out, pool_w, pool_scale, ln_g, ln_b, moe_router_g_w, moe_router_g_b, moe_router_e_w, moe_router_e_b, moe_w_gate_up, moe_w_down):
    wts = {
        "w_in": lru_w_in[0].astype(BF16), "conv_w": lru_conv_w[0], "conv_b": lru_conv_b[0],
        "w_a": lru_w_a[0].astype(BF16), "b_a": lru_b_a[0], "w_x": lru_w_x[0].astype(BF16), "b_x": lru_b_x[0],
        "lam": lru_lambda[0], "w_out": lru_w_out[0].astype(BF16),
        "pool_w": pool_w[0].astype(BF16), "pool_scale": pool_scale[0], "ln_g": ln_g, "ln_b": ln_b,
        "router": [_router_weights(moe_router_g_w[i], moe_router_g_b[i], moe_router_e_w[i], moe_router_e_b[i])
                   for i in range(DEPTH)],
        "wgu": moe_w_gate_up.astype(BF16), "wd": moe_w_down.astype(BF16),
    }
    bp = x_prompt.shape[0]
    dt = x_prompt.dtype
    y_p, h_p, c_p, p_p = _trunk(
        x_prompt, jnp.zeros((bp, D_RNN), dt), jnp.zeros((bp, CONV_WIDTH - 1, D_RNN), dt),
        jnp.zeros((bp, POOL_STATE, D_MODEL), dt), 0, wts, ts=256, tm=256, sparse=True)
    y_s, h_s, c_s, p_s = _trunk(
        x_sample, state_lru_h[0], state_lru_conv[0], state_pool[0], PAST_LEN, wts,
        ts=x_sample.shape[1], tm=256, sparse=False)
    return (y_p, y_s, h_p, c_p, p_p, h_s, c_s, p_s)
```

```python
import functools

import jax
import jax.numpy as jnp
from jax import lax
from jax.experimental import pallas as pl
from jax.experimental.pallas import tpu as pltpu

F32 = jnp.float32
BF16 = jnp.bfloat16

D_MODEL = 2048
DEPTH = 2
D_RNN = D_MODEL
LRU_HEADS = 8
LRU_BLOCK = D_RNN // LRU_HEADS
CONV_WIDTH = 4
LRU_C = 8.0
POOL_WINDOWS = (2, 4, 8, 16)
POOL_GROUPS = 4
POOL_GROUP_DIM = D_MODEL // POOL_GROUPS
POOL_STATE = max(POOL_WINDOWS) - 1
MOE_GROUPS = 4
MOE_EXPERTS_PER_GROUP = 4
N_EXPERTS = MOE_GROUPS * MOE_EXPERTS_PER_GROUP
D_EXPERT = 512
DEEPNORM_ALPHA = (2.0 * DEPTH) ** 0.25
LN_EPS = 1e-5
PAST_LEN = 2048

LANES = 128
SUBLANES = 8
ROUTER_LANES = LANES
ROW_WIDTH = D_MODEL + ROUTER_LANES
EXPERT_LANE0 = MOE_GROUPS
CLASS_LANE = 0
RANK_LANE = 1
GATE_LANE0 = 2
PAIRS_PER_GROUP = 6
N_CLASSES = MOE_GROUPS * PAIRS_PER_GROUP
PAIR_SLOTS = ((0, 1), (0, 2), (0, 3), (1, 3), (1, 2), (3, 2))
PAIR_ORDER = (0, 1, 2, 4, 3, 5)
VMEM_LIMIT = 56 * 1024 * 1024
MOE_TILE = 256


def _cparams(*sem):
    return pltpu.CompilerParams(dimension_semantics=sem, vmem_limit_bytes=VMEM_LIMIT)


def _layer_norm(z, g, b):
    mu = jnp.mean(z, axis=-1, keepdims=True)
    zc = z - mu
    var = jnp.mean(zc * zc, axis=-1, keepdims=True)
    return zc * lax.rsqrt(var + LN_EPS) * g + b


def _route(x1, wr_a_ref, wr_b_ref, rb_ref, cnt_ref, first):
    @pl.when(first)
    def _():
        cnt_ref[...] = jnp.zeros_like(cnt_ref)

    hi = x1.astype(BF16)
    lo = (x1 - hi.astype(F32)).astype(BF16)
    c = jnp.dot(hi, wr_a_ref[...], preferred_element_type=F32)
    d = jnp.dot(lo, wr_b_ref[...], preferred_element_type=F32)
    logits = c[:, :ROUTER_LANES] + c[:, ROUTER_LANES:] + d + rb_ref[...]
    tm = logits.shape[0]
    lane = lax.broadcasted_iota(jnp.int32, (tm, ROUTER_LANES), 1)
    neg = jnp.float32(-jnp.inf)
    is_g = lane < MOE_GROUPS
    lg = jnp.where(is_g, logits, neg)
    gmax = jnp.max(lg, axis=1, keepdims=True)
    gsel = jnp.min(jnp.where(lg == gmax, lane, ROUTER_LANES), axis=1, keepdims=True)
    p_sel = 1.0 / jnp.sum(jnp.where(is_g, jnp.exp(logits - gmax), 0.0), axis=1, keepdims=True)
    elane = lane - EXPERT_LANE0
    in_g = (elane >= 0) & (elane < N_EXPERTS) & ((elane >> 2) == gsel)
    le = jnp.where(in_g, logits, neg)
    t1 = jnp.max(le, axis=1, keepdims=True)
    i1 = jnp.min(jnp.where(le == t1, lane, ROUTER_LANES), axis=1, keepdims=True)
    le2 = jnp.where(lane == i1, neg, le)
    t2 = jnp.max(le2, axis=1, keepdims=True)
    i2 = jnp.min(jnp.where(le2 == t2, lane, ROUTER_LANES), axis=1, keepdims=True)
    e2 = jnp.exp(t2 - t1)
    inv = p_sel / (1.0 + e2)
    gates = jnp.where(lane == i1, inv, 0.0) + jnp.where(lane == i2, inv * e2, 0.0)
    base = EXPERT_LANE0 + MOE_EXPERTS_PER_GROUP * gsel
    pa = jnp.minimum(i1, i2) - base
    pb = jnp.maximum(i1, i2) - base
    cls = PAIRS_PER_GROUP * gsel + 3 * pa - ((pa * (pa - 1)) >> 1) + pb - pa - 1
    kp = cls - PAIRS_PER_GROUP * gsel
    kp = jnp.where(kp == 3, 4, jnp.where(kp == 4, 3, kp))
    pos = PAIRS_PER_GROUP * gsel + kp
    first_local = jnp.where(kp < 3, 0, jnp.where(kp < 5, 1, 3))
    i1_first = (i1 - base) == first_local
    g_first = jnp.where(i1_first, inv, inv * e2)
    g_second = jnp.where(i1_first, inv * e2, inv)
    onehot = lane == pos
    tri = (lax.broadcasted_iota(jnp.int32, (tm, tm), 0) >= lax.broadcasted_iota(jnp.int32, (tm, tm), 1))
    cum = jnp.dot(tri.astype(BF16), onehot.astype(BF16), preferred_element_type=F32)
    run = cnt_ref[0:1, :]
    rank = jnp.sum(jnp.where(onehot, cum + run, 0.0), axis=1, keepdims=True) - 1.0
    cnt_ref[...] = jnp.broadcast_to(run + cum[tm - 1:tm, :], cnt_ref.shape)
    for ln, val in ((CLASS_LANE, pos.astype(F32)), (RANK_LANE, rank), (GATE_LANE0, g_first), (GATE_LANE0 + 1, g_second)):
        gates = jnp.where(lane == ln, val, gates)
    return hi, gates


def _proj_kernel(x_ref, w_ref, o_ref):
    o_ref[...] = jnp.dot(x_ref[...].astype(BF16), w_ref[...], preferred_element_type=F32)


def _proj(x, w, tm, tn):
    t, k = x.shape
    n = w.shape[1]
    return pl.pallas_call(
        _proj_kernel,
        grid=(n // tn, t // tm),
        in_specs=[pl.BlockSpec((tm, k), lambda jn, im: (im, 0)),
                  pl.BlockSpec((k, tn), lambda jn, im: (0, jn))],
        out_specs=pl.BlockSpec((tm, tn), lambda jn, im: (im, jn)),
        out_shape=jax.ShapeDtypeStruct((t, n), F32),
        compiler_params=_cparams("arbitrary", "arbitrary"),
        name="lru_in_proj",
    )(x, w)


def _lru_kernel(gate_ref, u_ref, h0_ref, conv0_ref, cw_ref, cb_ref, wa_ref, ba_ref, wx_ref, bx_ref,
                lam_ref, y_ref, hl_ref, cn_ref, ubuf, hc, *, ts):
    j = pl.program_id(1)

    @pl.when(j == 0)
    def _():
        ubuf[0:SUBLANES, :] = jnp.zeros((SUBLANES, D_RNN), F32)
        ubuf[SUBLANES - (CONV_WIDTH - 1):SUBLANES, :] = conv0_ref[...]
        hc[...] = h0_ref[...]

    u = u_ref[...]
    ubuf[SUBLANES:SUBLANES + ts, :] = u
    uc = cb_ref[...] + ubuf[SUBLANES - 3:SUBLANES - 3 + ts, :] * cw_ref[0:1, :]
    uc = uc + ubuf[SUBLANES - 2:SUBLANES - 2 + ts, :] * cw_ref[1:2, :]
    uc = uc + ubuf[SUBLANES - 1:SUBLANES - 1 + ts, :] * cw_ref[2:3, :]
    uc = uc + u * cw_ref[3:4, :]
    cn_ref[...] = ubuf[SUBLANES + ts - (CONV_WIDTH - 1):SUBLANES + ts, :]
    ubuf[0:SUBLANES, :] = ubuf[ts:ts + SUBLANES, :]

    ucb = uc.astype(BF16)
    rs, igs = [], []
    for h in range(LRU_HEADS):
        blk = ucb[:, h * LRU_BLOCK:(h + 1) * LRU_BLOCK]
        rs.append(jnp.dot(blk, wa_ref[h], preferred_element_type=F32))
        igs.append(jnp.dot(blk, wx_ref[h], preferred_element_type=F32))
    r = jax.nn.sigmoid(jnp.concatenate(rs, axis=1) + ba_ref[...])
    ig = jax.nn.sigmoid(jnp.concatenate(igs, axis=1) + bx_ref[...])
    log_a = (-LRU_C) * r * jax.nn.softplus(-lam_ref[...])
    a = jnp.exp(log_a)
    b = jnp.sqrt(-jnp.tanh(log_a) * (a * a + 1.0)) * ig * uc

    row = lax.broadcasted_iota(jnp.int32, (ts, D_RNN), 0) & (SUBLANES - 1)
    for s in (1, 2, 4):
        keep = row >= s
        a_s = pltpu.roll(a, s, 0)
        b_s = pltpu.roll(b, s, 0)
        b = jnp.where(keep, a * b_s + b, b)
        a = jnp.where(keep, a * a_s, a)
    hprev = hc[...]
    hs = []
    for g in range(ts // SUBLANES):
        hg = a[g * SUBLANES:(g + 1) * SUBLANES] * hprev + b[g * SUBLANES:(g + 1) * SUBLANES]
        hs.append(hg)
        hprev = hg[SUBLANES - 1:SUBLANES]
    h = jnp.concatenate(hs, axis=0)
    hc[...] = hprev
    hl_ref[...] = hprev
    y_ref[...] = (h * jax.nn.gelu(gate_ref[...])).astype(BF16)


def _lru_scan(proj, h0, conv0, cw, cb, wa, ba, wx, bx, lam, ts):
    bsz, s, _ = proj.shape
    row2 = lambda b, j: (0, 0)
    return pl.pallas_call(
        functools.partial(_lru_kernel, ts=ts),
        grid=(bsz, s // ts),
        in_specs=[pl.BlockSpec((None, ts, D_RNN), lambda b, j: (b, j, 0)),
                  pl.BlockSpec((None, ts, D_RNN), lambda b, j: (b, j, 1)),
                  pl.BlockSpec((None, 1, D_RNN), lambda b, j: (b, 0, 0)),
                  pl.BlockSpec((None, CONV_WIDTH - 1, D_RNN), lambda b, j: (b, 0, 0)),
                  pl.BlockSpec((CONV_WIDTH, D_RNN), row2),
                  pl.BlockSpec((1, D_RNN), row2),
                  pl.BlockSpec((LRU_HEADS, LRU_BLOCK, LRU_BLOCK), lambda b, j: (0, 0, 0)),
                  pl.BlockSpec((1, D_RNN), row2),
                  pl.BlockSpec((LRU_HEADS, LRU_BLOCK, LRU_BLOCK), lambda b, j: (0, 0, 0)),
                  pl.BlockSpec((1, D_RNN), row2),
                  pl.BlockSpec((1, D_RNN), row2)],
        out_specs=[pl.BlockSpec((None, ts, D_RNN), lambda b, j: (b, j, 0)),
                   pl.BlockSpec((None, 1, D_RNN), lambda b, j: (b, 0, 0)),
                   pl.BlockSpec((None, CONV_WIDTH - 1, D_RNN), lambda b, j: (b, 0, 0))],
        out_shape=[jax.ShapeDtypeStruct((bsz, s, D_RNN), BF16),
                   jax.ShapeDtypeStruct((bsz, 1, D_RNN), F32),
                   jax.ShapeDtypeStruct((bsz, CONV_WIDTH - 1, D_RNN), F32)],
        scratch_shapes=[pltpu.VMEM((SUBLANES + ts, D_RNN), F32), pltpu.VMEM((1, D_RNN), F32)],
        compiler_params=_cparams("arbitrary", "arbitrary"),
        name="lru_scan",
    )(proj, proj, h0, conv0, cw, cb, wa, ba, wx, bx, lam)


def _out_kernel(y_ref, w_ref, x_ref, g_ref, b_ref, wr_a_ref, wr_b_ref, rb_ref,
                x1_ref, x1b_ref, gates_ref, cnt_ref):
    mix = jnp.dot(y_ref[...], w_ref[...], preferred_element_type=F32)
    x1 = _layer_norm(DEEPNORM_ALPHA * x_ref[...] + mix, g_ref[...], b_ref[...])
    x1b_ref[...], route = _route(x1, wr_a_ref, wr_b_ref, rb_ref, cnt_ref, pl.program_id(0) == 0)
    gates_ref[...] = route
    x1_ref[:, :D_MODEL] = x1
    x1_ref[:, D_MODEL:] = route


def _out_proj(y, w, x, g, b, wr_a, wr_b, rb, tm):
    t = x.shape[0]
    full = lambda i: (0, 0)
    tile = lambda i: (i, 0)
    return pl.pallas_call(
        _out_kernel,
        grid=(t // tm,),
        in_specs=[pl.BlockSpec((tm, D_RNN), tile),
                  pl.BlockSpec((D_RNN, D_MODEL), full),
                  pl.BlockSpec((tm, D_MODEL), tile),
                  pl.BlockSpec((1, D_MODEL), full),
                  pl.BlockSpec((1, D_MODEL), full),
                  pl.BlockSpec((D_MODEL, 2 * ROUTER_LANES), full),
                  pl.BlockSpec((D_MODEL, ROUTER_LANES), full),
                  pl.BlockSpec((1, ROUTER_LANES), full)],
        out_specs=[pl.BlockSpec((tm, ROW_WIDTH), tile),
                   pl.BlockSpec((tm, D_MODEL), tile),
                   pl.BlockSpec((tm, ROUTER_LANES), tile),
                   pl.BlockSpec((SUBLANES, ROUTER_LANES), full)],
        out_shape=[jax.ShapeDtypeStruct((t, ROW_WIDTH), F32),
                   jax.ShapeDtypeStruct((t, D_MODEL), BF16),
                   jax.ShapeDtypeStruct((t, ROUTER_LANES), F32),
                   jax.ShapeDtypeStruct((SUBLANES, ROUTER_LANES), F32)],
        compiler_params=_cparams("arbitrary"),
        name="lru_out_ln_route",
    )(y, w, x, g, b, wr_a, wr_b, rb)


def _pool_kernel(x_ref, p0_ref, pw_ref, sc_ref, g_ref, b_ref, wr_a_ref, wr_b_ref, rb_ref,
                 x1_ref, x1b_ref, gates_ref, pn_ref, cnt_ref, ext, *, ts, pos0):
    j = pl.program_id(1)
    hist = POOL_STATE + 1

    @pl.when(j == 0)
    def _():
        ext[0:hist, :] = p0_ref[...]

    x = x_ref[...]
    ext[hist:hist + ts, :] = x
    pos = (pos0 + j * ts + lax.broadcasted_iota(jnp.int32, (ts, 1), 0)).astype(F32)
    outs = []
    for g, w in enumerate(POOL_WINDOWS):
        lo, hi = g * POOL_GROUP_DIM, (g + 1) * POOL_GROUP_DIM
        acc = x[:, lo:hi]
        for k in range(1, w):
            acc = acc + ext[hist - k:hist - k + ts, lo:hi]
        mean = acc / jnp.minimum(jnp.float32(w), pos + 1.0)
        diff = (mean - x[:, lo:hi]).astype(BF16)
        outs.append(jnp.dot(diff, pw_ref[g], preferred_element_type=F32))
    mix = jnp.concatenate(outs, axis=1) * sc_ref[...]
    pn_ref[...] = ext[ts:ts + hist, :]
    ext[0:hist, :] = ext[ts:ts + hist, :]
    x1 = _layer_norm(DEEPNORM_ALPHA * x + mix, g_ref[...], b_ref[...])
    first = (pl.program_id(0) == 0) & (j == 0)
    x1b_ref[...], route = _route(x1, wr_a_ref, wr_b_ref, rb_ref, cnt_ref, first)
    gates_ref[...] = route
    x1_ref[:, :D_MODEL] = x1
    x1_ref[:, D_MODEL:] = route


def _pool_mix(x, p0, pw, sc, g, b, wr_a, wr_b, rb, ts, pos0):
    bsz, s, _ = x.shape
    hist = POOL_STATE + 1
    full = lambda bb, j: (0, 0)
    tile = lambda bb, j: (bb, j, 0)
    return pl.pallas_call(
        functools.partial(_pool_kernel, ts=ts, pos0=pos0),
        grid=(bsz, s // ts),
        in_specs=[pl.BlockSpec((None, ts, D_MODEL), tile),
                  pl.BlockSpec((None, hist, D_MODEL), lambda bb, j: (bb, 0, 0)),
                  pl.BlockSpec((POOL_GROUPS, POOL_GROUP_DIM, POOL_GROUP_DIM), lambda bb, j: (0, 0, 0)),
                  pl.BlockSpec((1, D_MODEL), full),
                  pl.BlockSpec((1, D_MODEL), full),
                  pl.BlockSpec((1, D_MODEL), full),
                  pl.BlockSpec((D_MODEL, 2 * ROUTER_LANES), full),
                  pl.BlockSpec((D_MODEL, ROUTER_LANES), full),
                  pl.BlockSpec((1, ROUTER_LANES), full)],
        out_specs=[pl.BlockSpec((None, ts, ROW_WIDTH), tile),
                   pl.BlockSpec((None, ts, D_MODEL), tile),
                   pl.BlockSpec((None, ts, ROUTER_LANES), tile),
                   pl.BlockSpec((None, hist, D_MODEL), lambda bb, j: (bb, 0, 0)),
                   pl.BlockSpec((SUBLANES, ROUTER_LANES), full)],
        out_shape=[jax.ShapeDtypeStruct((bsz, s, ROW_WIDTH), F32),
                   jax.ShapeDtypeStruct((bsz, s, D_MODEL), BF16),
                   jax.ShapeDtypeStruct((bsz, s, ROUTER_LANES), F32),
                   jax.ShapeDtypeStruct((bsz, hist, D_MODEL), F32),
                   jax.ShapeDtypeStruct((SUBLANES, ROUTER_LANES), F32)],
        scratch_shapes=[pltpu.VMEM((hist + ts, D_MODEL), F32)],
        compiler_params=_cparams("arbitrary", "arbitrary"),
        name="pool_mix_ln_route",
    )(x, p0, pw, sc, g, b, wr_a, wr_b, rb)


def _moe_kernel(xb_ref, gates_ref, wgu_ref, wd_ref, x1_ref, g_ref, b_ref, o_ref, acc):
    e = pl.program_id(1)

    @pl.when(e == 0)
    def _():
        acc[...] = jnp.zeros_like(acc)

    hu = jnp.dot(xb_ref[...], wgu_ref[...], preferred_element_type=F32)
    hg, hv = hu[:, :D_EXPERT], hu[:, D_EXPERT:]
    gates = gates_ref[...]
    lane = lax.broadcasted_iota(jnp.int32, gates.shape, 1)
    gate = jnp.sum(jnp.where(lane == e + EXPERT_LANE0, gates, 0.0), axis=1, keepdims=True)
    act = (jax.nn.silu(hg) * hv * gate).astype(BF16)
    acc[...] += jnp.dot(act, wd_ref[...], preferred_element_type=F32)

    @pl.when(e == N_EXPERTS - 1)
    def _():
        o_ref[...] = _layer_norm(DEEPNORM_ALPHA * x1_ref[...] + acc[...], g_ref[...], b_ref[...])


def _moe(xb, gates, wgu, wd, layer, x1, g, b, tm):
    t = xb.shape[0]
    tile = lambda i, e: (i, 0)
    full = lambda i, e: (0, 0)
    return pl.pallas_call(
        _moe_kernel,
        grid=(t // tm, N_EXPERTS),
        in_specs=[pl.BlockSpec((tm, D_MODEL), tile),
                  pl.BlockSpec((tm, ROUTER_LANES), tile),
                  pl.BlockSpec((None, None, D_MODEL, 2 * D_EXPERT), lambda i, e: (layer, e, 0, 0)),
                  pl.BlockSpec((None, None, D_EXPERT, D_MODEL), lambda i, e: (layer, e, 0, 0)),
                  pl.BlockSpec((tm, D_MODEL), tile),
                  pl.BlockSpec((1, D_MODEL), full),
                  pl.BlockSpec((1, D_MODEL), full)],
        out_specs=pl.BlockSpec((tm, D_MODEL), tile),
        out_shape=jax.ShapeDtypeStruct((t, D_MODEL), F32),
        scratch_shapes=[pltpu.VMEM((tm, D_MODEL), F32)],
        compiler_params=_cparams("arbitrary", "arbitrary"),
        name="moe_ffn_ln",
    )(xb, gates, wgu, wd, x1, g, b)


def _dispatch(route, counts, tm):
    t = route.shape[0]
    n_tiles = t // tm + N_CLASSES
    cpos = jnp.arange(N_CLASSES, dtype=jnp.int32)
    slots = jnp.array(PAIR_SLOTS, jnp.int32)
    exp_ab = MOE_EXPERTS_PER_GROUP * (cpos // PAIRS_PER_GROUP)[:, None] + slots[cpos % PAIRS_PER_GROUP]
    pos = route[:, CLASS_LANE].astype(jnp.int32)
    rank = route[:, RANK_LANE].astype(jnp.int32)
    counts = counts[0, :N_CLASSES].astype(jnp.int32)
    padded = ((counts + tm - 1) // tm) * tm
    ends = jnp.cumsum(padded)
    offs = ends - padded
    dest = offs[pos] + rank
    src = jnp.zeros((n_tiles * tm,), jnp.int32).at[dest].set(jnp.arange(t, dtype=jnp.int32))
    n_valid_tiles = ends[-1] // tm
    tile_start = jnp.arange(n_tiles, dtype=jnp.int32) * tm
    tile_pos = jnp.sum((tile_start[:, None] >= ends[None, :]).astype(jnp.int32), axis=1)
    live = jnp.arange(n_tiles) < n_valid_tiles
    tile_pos = jnp.where(live, tile_pos, tile_pos[n_valid_tiles - 1])
    tile_rows = jnp.where(live, jnp.clip(offs[tile_pos] + counts[tile_pos] - tile_start, 0, tm), 0)
    return (src.reshape(n_tiles, 1, tm), exp_ab[tile_pos, 0], exp_ab[tile_pos, 1],
            tile_rows.astype(jnp.int32), n_valid_tiles.reshape(1).astype(jnp.int32))


def _moe_sparse_kernel(ea_ref, eb_ref, rows_ref, nvt_ref,
                       src_ref, srcn_ref, x1_hbm, wgu_a, wgu_b, wd_a, wd_b, g_ref, b_ref, spill_in,
                       x2_hbm, spill_hbm, xbuf, obuf, gsem, ssem, *, tm):
    del ea_ref, eb_ref, spill_in
    i = pl.program_id(0)
    nvt = nvt_ref[0]
    slot = i % 2

    def gather_copy(tok, r, s):
        return pltpu.make_async_copy(x1_hbm.at[pl.ds(tok, 1)], xbuf.at[s, pl.ds(r, 1)], gsem.at[s])

    def start_gather(idx_ref, s):
        def body(r, c):
            gather_copy(idx_ref[0, r], r, s).start()
            return c
        lax.fori_loop(0, tm, body, 0, unroll=8)

    def wait_gather(s):
        pltpu.make_async_copy(x1_hbm.at[pl.ds(0, tm)], xbuf.at[s], gsem.at[s]).wait()

    def wait_scatter(s):
        pltpu.make_async_copy(obuf.at[s], x2_hbm.at[pl.ds(0, tm)], ssem.at[s]).wait()

    @pl.when(i == 0)
    def _():
        start_gather(src_ref, 0)

    @pl.when(i + 1 < nvt)
    def _():
        start_gather(srcn_ref, 1 - slot)

    @pl.when(i < nvt)
    def _():
        wait_gather(slot)

        @pl.when(i >= 2)
        def _():
            wait_scatter(slot)

        xs = xbuf[slot, :, :D_MODEL]
        xb = xs.astype(BF16)
        gab = xbuf[slot, :, D_MODEL:]
        acc = None
        for col, (wgu, wd) in enumerate(((wgu_a, wd_a), (wgu_b, wd_b))):
            hu = jnp.dot(xb, wgu[...], preferred_element_type=F32)
            act = (jax.nn.silu(hu[:, :D_EXPERT]) * hu[:, D_EXPERT:] * gab[:, GATE_LANE0 + col:GATE_LANE0 + col + 1]).astype(BF16)
            out = jnp.dot(act, wd[...], preferred_element_type=F32)
            acc = out if acc is None else acc + out
        obuf[slot] = _layer_norm(DEEPNORM_ALPHA * xs + acc, g_ref[...], b_ref[...])

        rows = rows_ref[i]

        def put_row(r, c):
            pltpu.make_async_copy(obuf.at[slot, pl.ds(r, 1)], x2_hbm.at[pl.ds(src_ref[0, r], 1)], ssem.at[slot]).start()
            return c

        def put_pad(r, c):
            pltpu.make_async_copy(obuf.at[slot, pl.ds(r, 1)], spill_hbm.at[pl.ds(slot * tm + r, 1)], ssem.at[slot]).start()
            return c

        lax.fori_loop(0, rows, put_row, 0)
        lax.fori_loop(rows, tm, put_pad, 0)

    @pl.when(i == nvt - 1)
    def _():
        wait_scatter(slot)

        @pl.when(i >= 1)
        def _():
            wait_scatter(1 - slot)


def _moe_sparse(x1, route, counts, wgu, wd, layer, g, b, tm):
    t = x1.shape[0]
    src, tile_ea, tile_eb, tile_rows, nvt = _dispatch(route, counts, tm)
    n_tiles = src.shape[0]
    full = lambda i, *_: (0, 0)
    grid_spec = pltpu.PrefetchScalarGridSpec(
        num_scalar_prefetch=4,
        grid=(n_tiles,),
        in_specs=[pl.BlockSpec((None, 1, tm), lambda i, *_: (i, 0, 0), memory_space=pltpu.SMEM),
                  pl.BlockSpec((None, 1, tm), lambda i, *_: (jnp.minimum(i + 1, n_tiles - 1), 0, 0),
                               memory_space=pltpu.SMEM),
                  pl.BlockSpec(memory_space=pl.ANY),
                  pl.BlockSpec((None, None, D_MODEL, 2 * D_EXPERT), lambda i, ea, eb, *_: (layer, ea[i], 0, 0)),
                  pl.BlockSpec((None, None, D_MODEL, 2 * D_EXPERT), lambda i, ea, eb, *_: (layer, eb[i], 0, 0)),
                  pl.BlockSpec((None, None, D_EXPERT, D_MODEL), lambda i, ea, eb, *_: (layer, ea[i], 0, 0)),
                  pl.BlockSpec((None, None, D_EXPERT, D_MODEL), lambda i, ea, eb, *_: (layer, eb[i], 0, 0)),
                  pl.BlockSpec((1, D_MODEL), full),
                  pl.BlockSpec((1, D_MODEL), full),
                  pl.BlockSpec(memory_space=pl.ANY)],
        out_specs=[pl.BlockSpec(memory_space=pl.ANY), pl.BlockSpec(memory_space=pl.ANY)],
        scratch_shapes=[pltpu.VMEM((2, tm, ROW_WIDTH), F32), pltpu.VMEM((2, tm, D_MODEL), F32),
                        pltpu.SemaphoreType.DMA((2,)), pltpu.SemaphoreType.DMA((2,))],
    )
    x2, _ = pl.pallas_call(
        functools.partial(_moe_sparse_kernel, tm=tm),
        grid_spec=grid_spec,
        out_shape=[jax.ShapeDtypeStruct((t, D_MODEL), F32), jax.ShapeDtypeStruct((2 * tm, D_MODEL), F32)],
        input_output_aliases={13: 1},
        compiler_params=_cparams("arbitrary"),
        name="moe_sparse_ln",
    )(tile_ea, tile_eb, tile_rows, nvt, src, src, x1, wgu, wgu, wd, wd, g, b,
      jnp.zeros((2 * tm, D_MODEL), F32))
    return x2


def _router_weights(wg, bg, we, be):
    w = jnp.concatenate([wg, we], axis=1)
    w = jnp.pad(w, ((0, 0), (0, ROUTER_LANES - w.shape[1])))
    w_hi = w.astype(BF16)
    w_lo = (w - w_hi.astype(F32)).astype(BF16)
    rb = jnp.pad(jnp.concatenate([bg, be]), (0, ROUTER_LANES - MOE_GROUPS - N_EXPERTS))
    return jnp.concatenate([w_hi, w_lo], axis=1), w_hi, rb.reshape(1, ROUTER_LANES)


def _trunk(x, lru_h0, lru_conv0, pool0, pos0, wts, ts, tm, sparse):
    bsz, s, d = x.shape
    t = bsz * s
    row = lambda v: v.reshape(1, -1)
    proj = _proj(x.reshape(t, d), wts["w_in"], min(512, t), 1024).reshape(bsz, s, 2 * D_RNN)
    y, h_last, conv_new = _lru_scan(
        proj, lru_h0.reshape(bsz, 1, D_RNN), lru_conv0, wts["conv_w"], row(wts["conv_b"]),
        wts["w_a"], row(wts["b_a"]), wts["w_x"], row(wts["b_x"]), row(wts["lam"]), ts)
    x1, x1b, gates, counts = _out_proj(y.reshape(t, D_RNN), wts["w_out"], x.reshape(t, d),
                               row(wts["ln_g"][0, 0]), row(wts["ln_b"][0, 0]), *wts["router"][0], tm)
    def moe(layer, x1, x1b, gates, counts):
        g, b = row(wts["ln_g"][layer, 1]), row(wts["ln_b"][layer, 1])
        if sparse:
            return _moe_sparse(x1, gates, counts, wts["wgu"], wts["wd"], layer, g, b, MOE_TILE)
        return _moe(x1b, gates, wts["wgu"], wts["wd"], layer, x1[:, :D_MODEL], g, b, min(512, t))

    x2 = moe(0, x1, x1b, gates, counts)
    p0 = jnp.pad(pool0, ((0, 0), (1, 0), (0, 0)))
    x1, x1b, gates, pool_new, counts = _pool_mix(
        x2.reshape(bsz, s, d), p0, wts["pool_w"], row(wts["pool_scale"]),
        row(wts["ln_g"][1, 0]), row(wts["ln_b"][1, 0]), *wts["router"][1], ts, pos0)
    y_out = moe(1, x1.reshape(t, ROW_WIDTH), x1b.reshape(t, d), gates.reshape(t, ROUTER_LANES), counts)
    return (y_out.reshape(bsz, s, d), h_last.reshape(1, bsz, D_RNN), conv_new[None],
            pool_new[None, :, 1:, :])


def kernel(x_prompt, x_sample, state_lru_h, state_lru_conv, state_pool, lru_w_in, lru_conv_w, lru_conv_b, lru_w_a, lru_b_a, lru_w_x, lru_b_x, lru_lambda, lru_w_out, pool_w, pool_scale, ln_g, ln_b, moe_router_g_w, moe_router_g_b, moe_router_e_w, moe_router_e_b, moe_w_gate_up, moe_w_down):
    wts = {
        "w_in": lru_w_in[0].astype(BF16), "conv_w": lru_conv_w[0], "conv_b": lru_conv_b[0],
        "w_a": lru_w_a[0].astype(BF16), "b_a": lru_b_a[0], "w_x": lru_w_x[0].astype(BF16), "b_x": lru_b_x[0],
        "lam": lru_lambda[0], "w_out": lru_w_out[0].astype(BF16),
        "pool_w": pool_w[0].astype(BF16), "pool_scale": pool_scale[0], "ln_g": ln_g, "ln_b": ln_b,
        "router": [_router_weights(moe_router_g_w[i], moe_router_g_b[i], moe_router_e_w[i], moe_router_e_b[i])
                   for i in range(DEPTH)],
        "wgu": moe_w_gate_up.astype(BF16), "wd": moe_w_down.astype(BF16),
    }
    bp = x_prompt.shape[0]
    dt = x_prompt.dtype
    y_p, h_p, c_p, p_p = _trunk(
        x_prompt, jnp.zeros((bp, D_RNN), dt), jnp.zeros((bp, CONV_WIDTH - 1, D_RNN), dt),
        jnp.zeros((bp, POOL_STATE, D_MODEL), dt), 0, wts, ts=256, tm=256, sparse=True)
    y_s, h_s, c_s, p_s = _trunk(
        x_sample, state_lru_h[0], state_lru_conv[0], state_pool[0], PAST_LEN, wts,
        ts=x_sample.shape[1], tm=256, sparse=False)
    return (y_p, y_s, h_p, c_p, p_p, h_s, c_s, p_s)
```

```python
import functools

import jax
import jax.numpy as jnp
from jax import lax
from jax.experimental import pallas as pl
from jax.experimental.pallas import tpu as pltpu

F32 = jnp.float32
BF16 = jnp.bfloat16

D_MODEL = 2048
DEPTH = 2
D_RNN = D_MODEL
LRU_HEADS = 8
LRU_BLOCK = D_RNN // LRU_HEADS
CONV_WIDTH = 4
LRU_C = 8.0
POOL_WINDOWS = (2, 4, 8, 16)
POOL_GROUPS = 4
POOL_GROUP_DIM = D_MODEL // POOL_GROUPS
POOL_STATE = max(POOL_WINDOWS) - 1
MOE_GROUPS = 4
MOE_EXPERTS_PER_GROUP = 4
N_EXPERTS = MOE_GROUPS * MOE_EXPERTS_PER_GROUP
D_EXPERT = 512
DEEPNORM_ALPHA = (2.0 * DEPTH) ** 0.25
LN_EPS = 1e-5
PAST_LEN = 2048

LANES = 128
SUBLANES = 8
ROUTER_LANES = LANES
ROW_WIDTH = D_MODEL + ROUTER_LANES
EXPERT_LANE0 = MOE_GROUPS
CLASS_LANE = 0
RANK_LANE = 1
GATE_LANE0 = 2
PAIRS_PER_GROUP = 6
N_CLASSES = MOE_GROUPS * PAIRS_PER_GROUP
PAIR_SLOTS = ((0, 1), (0, 2), (0, 3), (1, 3), (1, 2), (3, 2))
PAIR_ORDER = (0, 1, 2, 4, 3, 5)
VMEM_LIMIT = 56 * 1024 * 1024
MOE_TILE = 256


def _cparams(*sem):
    return pltpu.CompilerParams(dimension_semantics=sem, vmem_limit_bytes=VMEM_LIMIT)


def _layer_norm(z, g, b):
    mu = jnp.mean(z, axis=-1, keepdims=True)
    zc = z - mu
    var = jnp.mean(zc * zc, axis=-1, keepdims=True)
    return zc * lax.rsqrt(var + LN_EPS) * g + b


def _route(x1, wr_a_ref, wr_b_ref, rb_ref, cnt_ref, first):
    @pl.when(first)
    def _():
        cnt_ref[...] = jnp.zeros_like(cnt_ref)

    hi = x1.astype(BF16)
    lo = (x1 - hi.astype(F32)).astype(BF16)
    c = jnp.dot(hi, wr_a_ref[...], preferred_element_type=F32)
    d = jnp.dot(lo, wr_b_ref[...], preferred_element_type=F32)
    logits = c[:, :ROUTER_LANES] + c[:, ROUTER_LANES:] + d + rb_ref[...]
    tm = logits.shape[0]
    lane = lax.broadcasted_iota(jnp.int32, (tm, ROUTER_LANES), 1)
    neg = jnp.float32(-jnp.inf)
    is_g = lane < MOE_GROUPS
    lg = jnp.where(is_g, logits, neg)
    gmax = jnp.max(lg, axis=1, keepdims=True)
    gsel = jnp.min(jnp.where(lg == gmax, lane, ROUTER_LANES), axis=1, keepdims=True)
    p_sel = 1.0 / jnp.sum(jnp.where(is_g, jnp.exp(logits - gmax), 0.0), axis=1, keepdims=True)
    elane = lane - EXPERT_LANE0
    in_g = (elane >= 0) & (elane < N_EXPERTS) & ((elane >> 2) == gsel)
    le = jnp.where(in_g, logits, neg)
    t1 = jnp.max(le, axis=1, keepdims=True)
    i1 = jnp.min(jnp.where(le == t1, lane, ROUTER_LANES), axis=1, keepdims=True)
    le2 = jnp.where(lane == i1, neg, le)
    t2 = jnp.max(le2, axis=1, keepdims=True)
    i2 = jnp.min(jnp.where(le2 == t2, lane, ROUTER_LANES), axis=1, keepdims=True)
    e2 = jnp.exp(t2 - t1)
    inv = p_sel / (1.0 + e2)
    gates = jnp.where(lane == i1, inv, 0.0) + jnp.where(lane == i2, inv * e2, 0.0)
    base = EXPERT_LANE0 + MOE_EXPERTS_PER_GROUP * gsel
    pa = jnp.minimum(i1, i2) - base
    pb = jnp.maximum(i1, i2) - base
    cls = PAIRS_PER_GROUP * gsel + 3 * pa - ((pa * (pa - 1)) >> 1) + pb - pa - 1
    kp = cls - PAIRS_PER_GROUP * gsel
    kp = jnp.where(kp == 3, 4, jnp.where(kp == 4, 3, kp))
    pos = PAIRS_PER_GROUP * gsel + kp
    first_local = jnp.where(kp < 3, 0, jnp.where(kp < 5, 1, 3))
    i1_first = (i1 - base) == first_local
    g_first = jnp.where(i1_first, inv, inv * e2)
    g_second = jnp.where(i1_first, inv * e2, inv)
    onehot = lane == pos
    tri = (lax.broadcasted_iota(jnp.int32, (tm, tm), 0) >= lax.broadcasted_iota(jnp.int32, (tm, tm), 1))
    cum = jnp.dot(tri.astype(BF16), onehot.astype(BF16), preferred_element_type=F32)
    run = cnt_ref[0:1, :]
    rank = jnp.sum(jnp.where(onehot, cum + run, 0.0), axis=1, keepdims=True) - 1.0
    cnt_ref[...] = jnp.broadcast_to(run + cum[tm - 1:tm, :], cnt_ref.shape)
    for ln, val in ((CLASS_LANE, pos.astype(F32)), (RANK_LANE, rank), (GATE_LANE0, g_first), (GATE_LANE0 + 1, g_second)):
        gates = jnp.where(lane == ln, val, gates)
    return gates


def _proj_kernel(x_ref, w_ref, o_ref):
    o_ref[...] = jnp.dot(x_ref[...].astype(BF16), w_ref[...], preferred_element_type=F32)


def _proj(x, w, tm, tn):
    t, k = x.shape
    n = w.shape[1]
    return pl.pallas_call(
        _proj_kernel,
        grid=(n // tn, t // tm),
        in_specs=[pl.BlockSpec((tm, k), lambda jn, im: (im, 0)),
                  pl.BlockSpec((k, tn), lambda jn, im: (0, jn))],
        out_specs=pl.BlockSpec((tm, tn), lambda jn, im: (im, jn)),
        out_shape=jax.ShapeDtypeStruct((t, n), F32),
        compiler_params=_cparams("arbitrary", "arbitrary"),
        name="lru_in_proj",
    )(x, w)


def _lru_kernel(gate_ref, u_ref, h0_ref, conv0_ref, cw_ref, cb_ref, wa_ref, ba_ref, wx_ref, bx_ref,
                lam_ref, y_ref, hl_ref, cn_ref, ubuf, hc, *, ts):
    j = pl.program_id(1)

    @pl.when(j == 0)
    def _():
        ubuf[0:SUBLANES, :] = jnp.zeros((SUBLANES, D_RNN), F32)
        ubuf[SUBLANES - (CONV_WIDTH - 1):SUBLANES, :] = conv0_ref[...]
        hc[...] = h0_ref[...]

    u = u_ref[...]
    ubuf[SUBLANES:SUBLANES + ts, :] = u
    uc = cb_ref[...] + ubuf[SUBLANES - 3:SUBLANES - 3 + ts, :] * cw_ref[0:1, :]
    uc = uc + ubuf[SUBLANES - 2:SUBLANES - 2 + ts, :] * cw_ref[1:2, :]
    uc = uc + ubuf[SUBLANES - 1:SUBLANES - 1 + ts, :] * cw_ref[2:3, :]
    uc = uc + u * cw_ref[3:4, :]
    cn_ref[...] = ubuf[SUBLANES + ts - (CONV_WIDTH - 1):SUBLANES + ts, :]
    ubuf[0:SUBLANES, :] = ubuf[ts:ts + SUBLANES, :]

    ucb = uc.astype(BF16)
    rs, igs = [], []
    for h in range(LRU_HEADS):
        blk = ucb[:, h * LRU_BLOCK:(h + 1) * LRU_BLOCK]
        rs.append(jnp.dot(blk, wa_ref[h], preferred_element_type=F32))
        igs.append(jnp.dot(blk, wx_ref[h], preferred_element_type=F32))
    r = jax.nn.sigmoid(jnp.concatenate(rs, axis=1) + ba_ref[...])
    ig = jax.nn.sigmoid(jnp.concatenate(igs, axis=1) + bx_ref[...])
    log_a = (-LRU_C) * r * jax.nn.softplus(-lam_ref[...])
    a = jnp.exp(log_a)
    b = jnp.sqrt(-jnp.tanh(log_a) * (a * a + 1.0)) * ig * uc

    row = lax.broadcasted_iota(jnp.int32, (ts, D_RNN), 0) & (SUBLANES - 1)
    for s in (1, 2, 4):
        keep = row >= s
        a_s = pltpu.roll(a, s, 0)
        b_s = pltpu.roll(b, s, 0)
        b = jnp.where(keep, a * b_s + b, b)
        a = jnp.where(keep, a * a_s, a)
    hprev = hc[...]
    hs = []
    for g in range(ts // SUBLANES):
        hg = a[g * SUBLANES:(g + 1) * SUBLANES] * hprev + b[g * SUBLANES:(g + 1) * SUBLANES]
        hs.append(hg)
        hprev = hg[SUBLANES - 1:SUBLANES]
    h = jnp.concatenate(hs, axis=0)
    hc[...] = hprev
    hl_ref[...] = hprev
    y_ref[...] = (h * jax.nn.gelu(gate_ref[...])).astype(BF16)


def _lru_scan(proj, h0, conv0, cw, cb, wa, ba, wx, bx, lam, ts):
    bsz, s, _ = proj.shape
    row2 = lambda b, j: (0, 0)
    return pl.pallas_call(
        functools.partial(_lru_kernel, ts=ts),
        grid=(bsz, s // ts),
        in_specs=[pl.BlockSpec((None, ts, D_RNN), lambda b, j: (b, j, 0)),
                  pl.BlockSpec((None, ts, D_RNN), lambda b, j: (b, j, 1)),
                  pl.BlockSpec((None, 1, D_RNN), lambda b, j: (b, 0, 0)),
                  pl.BlockSpec((None, CONV_WIDTH - 1, D_RNN), lambda b, j: (b, 0, 0)),
                  pl.BlockSpec((CONV_WIDTH, D_RNN), row2),
                  pl.BlockSpec((1, D_RNN), row2),
                  pl.BlockSpec((LRU_HEADS, LRU_BLOCK, LRU_BLOCK), lambda b, j: (0, 0, 0)),
                  pl.BlockSpec((1, D_RNN), row2),
                  pl.BlockSpec((LRU_HEADS, LRU_BLOCK, LRU_BLOCK), lambda b, j: (0, 0, 0)),
                  pl.BlockSpec((1, D_RNN), row2),
                  pl.BlockSpec((1, D_RNN), row2)],
        out_specs=[pl.BlockSpec((None, ts, D_RNN), lambda b, j: (b, j, 0)),
                   pl.BlockSpec((None, 1, D_RNN), lambda b, j: (b, 0, 0)),
                   pl.BlockSpec((None, CONV_WIDTH - 1, D_RNN), lambda b, j: (b, 0, 0))],
        out_shape=[jax.ShapeDtypeStruct((bsz, s, D_RNN), BF16),
                   jax.ShapeDtypeStruct((bsz, 1, D_RNN), F32),
                   jax.ShapeDtypeStruct((bsz, CONV_WIDTH - 1, D_RNN), F32)],
        scratch_shapes=[pltpu.VMEM((SUBLANES + ts, D_RNN), F32), pltpu.VMEM((1, D_RNN), F32)],
        compiler_params=_cparams("arbitrary", "arbitrary"),
        name="lru_scan",
    )(proj, proj, h0, conv0, cw, cb, wa, ba, wx, bx, lam)


def _emit_routed(x1, route, outs, dense_out):
    if dense_out:
        x1e_ref, x1b_ref, gates_ref = outs
        x1b_ref[...] = x1.astype(BF16)
        gates_ref[...] = route
    else:
        x1e_ref, meta_ref = outs
        meta_ref[...] = route.T[0:SUBLANES, :]
    x1e_ref[:, :D_MODEL] = x1
    x1e_ref[:, D_MODEL:] = route


def _out_kernel(y_ref, w_ref, x_ref, g_ref, b_ref, wr_a_ref, wr_b_ref, rb_ref, *outs, dense_out):
    *outs, cnt_ref = outs
    mix = jnp.dot(y_ref[...], w_ref[...], preferred_element_type=F32)
    x1 = _layer_norm(DEEPNORM_ALPHA * x_ref[...] + mix, g_ref[...], b_ref[...])
    route = _route(x1, wr_a_ref, wr_b_ref, rb_ref, cnt_ref, pl.program_id(0) == 0)
    _emit_routed(x1, route, outs, dense_out)


def _routed_out_specs(lead, rows, tile, dense_out):
    blk = (None,) * (len(lead) - 1)
    n_rows = lead[-1]
    specs = [pl.BlockSpec(blk + (rows, ROW_WIDTH), tile)]
    shapes = [jax.ShapeDtypeStruct(lead + (ROW_WIDTH,), F32)]
    if dense_out:
        specs += [pl.BlockSpec(blk + (rows, D_MODEL), tile), pl.BlockSpec(blk + (rows, ROUTER_LANES), tile)]
        shapes += [jax.ShapeDtypeStruct(lead + (D_MODEL,), BF16), jax.ShapeDtypeStruct(lead + (ROUTER_LANES,), F32)]
    else:
        specs += [pl.BlockSpec(blk + (None, SUBLANES, rows), lambda *i: tile(*i) + (0,))]
        shapes += [jax.ShapeDtypeStruct(lead[:-1] + (n_rows // rows, SUBLANES, rows), F32)]
    return specs, shapes


def _out_proj(y, w, x, g, b, wr_a, wr_b, rb, tm, dense_out):
    t = x.shape[0]
    full = lambda i: (0, 0)
    tile = lambda i: (i, 0)
    specs, shapes = _routed_out_specs((t,), tm, tile, dense_out)
    return pl.pallas_call(
        functools.partial(_out_kernel, dense_out=dense_out),
        grid=(t // tm,),
        in_specs=[pl.BlockSpec((tm, D_RNN), tile),
                  pl.BlockSpec((D_RNN, D_MODEL), full),
                  pl.BlockSpec((tm, D_MODEL), tile),
                  pl.BlockSpec((1, D_MODEL), full),
                  pl.BlockSpec((1, D_MODEL), full),
                  pl.BlockSpec((D_MODEL, 2 * ROUTER_LANES), full),
                  pl.BlockSpec((D_MODEL, ROUTER_LANES), full),
                  pl.BlockSpec((1, ROUTER_LANES), full)],
        out_specs=specs + [pl.BlockSpec((SUBLANES, ROUTER_LANES), full)],
        out_shape=shapes + [jax.ShapeDtypeStruct((SUBLANES, ROUTER_LANES), F32)],
        compiler_params=_cparams("arbitrary"),
        name="lru_out_ln_route",
    )(y, w, x, g, b, wr_a, wr_b, rb)


def _pool_kernel(x_ref, p0_ref, pw_ref, sc_ref, g_ref, b_ref, wr_a_ref, wr_b_ref, rb_ref,
                 *outs, ts, pos0, dense_out):
    *outs, pn_ref, cnt_ref, ext = outs
    j = pl.program_id(1)
    hist = POOL_STATE + 1

    @pl.when(j == 0)
    def _():
        ext[0:hist, :] = p0_ref[...]

    x = x_ref[...]
    ext[hist:hist + ts, :] = x
    pos = (pos0 + j * ts + lax.broadcasted_iota(jnp.int32, (ts, 1), 0)).astype(F32)
    mixes = []
    for g, w in enumerate(POOL_WINDOWS):
        lo, hi = g * POOL_GROUP_DIM, (g + 1) * POOL_GROUP_DIM
        acc = x[:, lo:hi]
        for k in range(1, w):
            acc = acc + ext[hist - k:hist - k + ts, lo:hi]
        mean = acc / jnp.minimum(jnp.float32(w), pos + 1.0)
        diff = (mean - x[:, lo:hi]).astype(BF16)
        mixes.append(jnp.dot(diff, pw_ref[g], preferred_element_type=F32))
    mix = jnp.concatenate(mixes, axis=1) * sc_ref[...]
    pn_ref[...] = ext[ts:ts + hist, :]
    ext[0:hist, :] = ext[ts:ts + hist, :]
    x1 = _layer_norm(DEEPNORM_ALPHA * x + mix, g_ref[...], b_ref[...])
    first = (pl.program_id(0) == 0) & (j == 0)
    route = _route(x1, wr_a_ref, wr_b_ref, rb_ref, cnt_ref, first)
    _emit_routed(x1, route, outs, dense_out)


def _pool_mix(x, bsz, s, row0, p0, pw, sc, g, b, wr_a, wr_b, rb, ts, pos0, dense_out):
    hist = POOL_STATE + 1
    full = lambda bb, j: (0, 0)
    tile = lambda bb, j: (bb, j, 0)
    specs, shapes = _routed_out_specs((bsz, s), ts, tile, dense_out)
    assert row0 % ts == 0 and s % ts == 0
    return pl.pallas_call(
        functools.partial(_pool_kernel, ts=ts, pos0=pos0, dense_out=dense_out),
        grid=(bsz, s // ts),
        in_specs=[pl.BlockSpec((ts, D_MODEL), lambda bb, j: (row0 // ts + bb * (s // ts) + j, 0)),
                  pl.BlockSpec((None, hist, D_MODEL), lambda bb, j: (bb, 0, 0)),
                  pl.BlockSpec((POOL_GROUPS, POOL_GROUP_DIM, POOL_GROUP_DIM), lambda bb, j: (0, 0, 0)),
                  pl.BlockSpec((1, D_MODEL), full),
                  pl.BlockSpec((1, D_MODEL), full),
                  pl.BlockSpec((1, D_MODEL), full),
                  pl.BlockSpec((D_MODEL, 2 * ROUTER_LANES), full),
                  pl.BlockSpec((D_MODEL, ROUTER_LANES), full),
                  pl.BlockSpec((1, ROUTER_LANES), full)],
        out_specs=specs + [pl.BlockSpec((None, hist, D_MODEL), lambda bb, j: (bb, 0, 0)),
                           pl.BlockSpec((SUBLANES, ROUTER_LANES), full)],
        out_shape=shapes + [jax.ShapeDtypeStruct((bsz, hist, D_MODEL), F32),
                            jax.ShapeDtypeStruct((SUBLANES, ROUTER_LANES), F32)],
        scratch_shapes=[pltpu.VMEM((hist + ts, D_MODEL), F32)],
        compiler_params=_cparams("arbitrary", "arbitrary"),
        name="pool_mix_ln_route",
    )(x, p0, pw, sc, g, b, wr_a, wr_b, rb)


def _moe_kernel(xb_ref, gates_ref, wgu_ref, wd_ref, x1_ref, g_ref, b_ref, o_ref, acc):
    e = pl.program_id(1)

    @pl.when(e == 0)
    def _():
        acc[...] = jnp.zeros_like(acc)

    hu = jnp.dot(xb_ref[...], wgu_ref[...], preferred_element_type=F32)
    hg, hv = hu[:, :D_EXPERT], hu[:, D_EXPERT:]
    gates = gates_ref[...]
    lane = lax.broadcasted_iota(jnp.int32, gates.shape, 1)
    gate = jnp.sum(jnp.where(lane == e + EXPERT_LANE0, gates, 0.0), axis=1, keepdims=True)
    act = (jax.nn.silu(hg) * hv * gate).astype(BF16)
    acc[...] += jnp.dot(act, wd_ref[...], preferred_element_type=F32)

    @pl.when(e == N_EXPERTS - 1)
    def _():
        o_ref[...] = _layer_norm(DEEPNORM_ALPHA * x1_ref[...] + acc[...], g_ref[...], b_ref[...])


def _moe(xb, gates, wgu, wd, layer, x1, g, b, tm):
    t = xb.shape[0]
    tile = lambda i, e: (i, 0)
    full = lambda i, e: (0, 0)
    return pl.pallas_call(
        _moe_kernel,
        grid=(t // tm, N_EXPERTS),
        in_specs=[pl.BlockSpec((tm, D_MODEL), tile),
                  pl.BlockSpec((tm, ROUTER_LANES), tile),
                  pl.BlockSpec((None, None, D_MODEL, 2 * D_EXPERT), lambda i, e: (layer, e, 0, 0)),
                  pl.BlockSpec((None, None, D_EXPERT, D_MODEL), lambda i, e: (layer, e, 0, 0)),
                  pl.BlockSpec((tm, D_MODEL), tile),
                  pl.BlockSpec((1, D_MODEL), full),
                  pl.BlockSpec((1, D_MODEL), full)],
        out_specs=pl.BlockSpec((tm, D_MODEL), tile),
        out_shape=jax.ShapeDtypeStruct((t, D_MODEL), F32),
        scratch_shapes=[pltpu.VMEM((tm, D_MODEL), F32)],
        compiler_params=_cparams("arbitrary", "arbitrary"),
        name="moe_ffn_ln",
    )(xb, gates, wgu, wd, x1, g, b)


def _dispatch(meta, counts, tm):
    t = meta.shape[0] * meta.shape[-1]
    n_tiles = t // tm
    n_steps_max = n_tiles + N_CLASSES - 1
    cpos = jnp.arange(N_CLASSES, dtype=jnp.int32)
    slots = jnp.array(PAIR_SLOTS, jnp.int32)
    exp_ab = MOE_EXPERTS_PER_GROUP * (cpos // PAIRS_PER_GROUP)[:, None] + slots[cpos % PAIRS_PER_GROUP]
    pos = meta[:, CLASS_LANE, :].reshape(t).astype(jnp.int32)
    rank = meta[:, RANK_LANE, :].reshape(t).astype(jnp.int32)
    counts = counts[0, :N_CLASSES].astype(jnp.int32)
    cend = jnp.cumsum(counts)
    cstart = cend - counts
    dest = cstart[pos] + rank
    src = jnp.zeros((t,), jnp.int32).at[dest].set(jnp.arange(t, dtype=jnp.int32))
    span = jnp.where(counts > 0, (cend - 1) // tm - cstart // tm + 1, 0)
    send = jnp.cumsum(span)
    sstart = send - span
    n_steps = send[-1]
    step = jnp.arange(n_steps_max, dtype=jnp.int32)
    live = step < n_steps
    cls = jnp.sum((step[:, None] >= send[None, :]).astype(jnp.int32), axis=1)
    cls = jnp.where(live, cls, cls[n_steps - 1])
    tile = jnp.where(live, cstart[cls] // tm + step - sstart[cls], n_tiles - 1)
    prev_tile = jnp.concatenate([jnp.full((1,), -1, jnp.int32), tile[:-1]])
    next_tile = jnp.concatenate([tile[1:], jnp.full((1,), -1, jnp.int32)])
    first = live & (tile != prev_tile)
    last = live & ((tile != next_tile) | (step == n_steps - 1))
    i32 = lambda v: v.astype(jnp.int32)
    return (src.reshape(n_tiles, 1, tm), i32(tile), exp_ab[cls, 0], exp_ab[cls, 1], i32(cls),
            i32(first), i32(last), i32(n_steps).reshape(1))


def _moe_sparse_kernel(tile_ref, ea_ref, eb_ref, cls_ref, first_ref, last_ref, nst_ref,
                       src_ref, srcn_ref, x1_hbm, wgu_a, wgu_b, wd_a, wd_b, g_ref, b_ref,
                       x2_hbm, xbuf, obuf, acc, gsem, ssem, *, tm, n_tiles):
    del ea_ref, eb_ref
    s = pl.program_id(0)
    live = s < nst_ref[0]
    tau = tile_ref[s]
    slot = tau % 2
    first = first_ref[s] == 1
    last = last_ref[s] == 1

    def start_gather(idx_ref, sl):
        for r in range(tm):
            pltpu.make_async_copy(x1_hbm.at[pl.ds(idx_ref[0, r], 1)], xbuf.at[sl, pl.ds(r, 1)], gsem.at[sl]).start()

    def wait_gather(sl):
        pltpu.make_async_copy(x1_hbm.at[pl.ds(0, tm)], xbuf.at[sl], gsem.at[sl]).wait()

    def wait_scatter(sl):
        pltpu.make_async_copy(obuf.at[sl], x2_hbm.at[pl.ds(0, tm)], ssem.at[sl]).wait()

    @pl.when(s == 0)
    def _():
        acc[...] = jnp.zeros_like(acc)
        start_gather(src_ref, 0)

    @pl.when(live & first)
    def _():
        wait_gather(slot)

        @pl.when(tau + 1 < n_tiles)
        def _():
            start_gather(srcn_ref, 1 - slot)

    @pl.when(live)
    def _():
        xb = xbuf[slot, :, :D_MODEL].astype(BF16)
        route = xbuf[slot, :, D_MODEL:]
        mine = route[:, CLASS_LANE:CLASS_LANE + 1] == cls_ref[s].astype(F32)
        out = None
        for col, (wgu, wd) in enumerate(((wgu_a, wd_a), (wgu_b, wd_b))):
            gate = jnp.where(mine, route[:, GATE_LANE0 + col:GATE_LANE0 + col + 1], 0.0)
            hu = jnp.dot(xb, wgu[...], preferred_element_type=F32)
            act = (jax.nn.silu(hu[:, :D_EXPERT]) * hu[:, D_EXPERT:] * gate).astype(BF16)
            o = jnp.dot(act, wd[...], preferred_element_type=F32)
            out = o if out is None else out + o

        acc[...] = jnp.where(first, out, acc[...] + out)

    @pl.when(live & last)
    def _():
        @pl.when(tau >= 2)
        def _():
            wait_scatter(slot)

        obuf[slot] = _layer_norm(DEEPNORM_ALPHA * xbuf[slot, :, :D_MODEL] + acc[...], g_ref[...], b_ref[...])
        for r in range(tm):
            pltpu.make_async_copy(obuf.at[slot, pl.ds(r, 1)], x2_hbm.at[pl.ds(src_ref[0, r], 1)], ssem.at[slot]).start()

    @pl.when(s == nst_ref[0] - 1)
    def _():
        wait_scatter(slot)

        @pl.when(tau >= 1)
        def _():
            wait_scatter(1 - slot)


def _moe_sparse(x1e, meta, counts, wgu, wd, layer, g, b, tm):
    t = x1e.shape[0]
    src, tile, exp_a, exp_b, cls, first, last, n_steps = _dispatch(meta, counts, tm)
    n_tiles = t // tm
    full = lambda s, *_: (0, 0)
    grid_spec = pltpu.PrefetchScalarGridSpec(
        num_scalar_prefetch=7,
        grid=(tile.shape[0],),
        in_specs=[pl.BlockSpec((None, 1, tm), lambda s, tl, *_: (tl[s], 0, 0), memory_space=pltpu.SMEM),
                  pl.BlockSpec((None, 1, tm), lambda s, tl, *_: (jnp.minimum(tl[s] + 1, n_tiles - 1), 0, 0),
                               memory_space=pltpu.SMEM),
                  pl.BlockSpec(memory_space=pl.ANY),
                  pl.BlockSpec((None, None, D_MODEL, 2 * D_EXPERT), lambda s, tl, ea, eb, *_: (layer, ea[s], 0, 0)),
                  pl.BlockSpec((None, None, D_MODEL, 2 * D_EXPERT), lambda s, tl, ea, eb, *_: (layer, eb[s], 0, 0)),
                  pl.BlockSpec((None, None, D_EXPERT, D_MODEL), lambda s, tl, ea, eb, *_: (layer, ea[s], 0, 0)),
                  pl.BlockSpec((None, None, D_EXPERT, D_MODEL), lambda s, tl, ea, eb, *_: (layer, eb[s], 0, 0)),
                  pl.BlockSpec((1, D_MODEL), full),
                  pl.BlockSpec((1, D_MODEL), full)],
        out_specs=pl.BlockSpec(memory_space=pl.ANY),
        scratch_shapes=[pltpu.VMEM((2, tm, ROW_WIDTH), F32), pltpu.VMEM((2, tm, D_MODEL), F32),
                        pltpu.VMEM((tm, D_MODEL), F32),
                        pltpu.SemaphoreType.DMA((2,)), pltpu.SemaphoreType.DMA((2,))],
    )
    return pl.pallas_call(
        functools.partial(_moe_sparse_kernel, tm=tm, n_tiles=n_tiles),
        grid_spec=grid_spec,
        out_shape=jax.ShapeDtypeStruct((t, D_MODEL), F32),
        compiler_params=_cparams("arbitrary"),
        name="moe_sparse_ln",
    )(tile, exp_a, exp_b, cls, first, last, n_steps, src, src, x1e, wgu, wgu, wd, wd, g, b)


def _router_weights(wg, bg, we, be):
    w = jnp.concatenate([wg, we], axis=1)
    w = jnp.pad(w, ((0, 0), (0, ROUTER_LANES - w.shape[1])))
    w_hi = w.astype(BF16)
    w_lo = (w - w_hi.astype(F32)).astype(BF16)
    rb = jnp.pad(jnp.concatenate([bg, be]), (0, ROUTER_LANES - MOE_GROUPS - N_EXPERTS))
    return jnp.concatenate([w_hi, w_lo], axis=1), w_hi, rb.reshape(1, ROUTER_LANES)


def _trunk(x, lru_h0, lru_conv0, pool0, pos0, wts, ts, tm, sparse):
    bsz, s, d = x.shape
    t = bsz * s
    row = lambda v: v.reshape(1, -1)

    def moe(layer, routed, counts):
        g, b = row(wts["ln_g"][layer, 1]), row(wts["ln_b"][layer, 1])
        if sparse:
            x1e, meta = routed
            meta = meta.reshape(-1, SUBLANES, meta.shape[-1])
            return _moe_sparse(x1e.reshape(t, ROW_WIDTH), meta, counts, wts["wgu"], wts["wd"], layer, g, b, MOE_TILE)
        x1e, x1b, gates = (v.reshape(t, -1) for v in routed)
        return _moe(x1b, gates, wts["wgu"], wts["wd"], layer, x1e[:, :D_MODEL], g, b, min(512, t))

    proj = _proj(x.reshape(t, d), wts["w_in"], min(512, t), 1024).reshape(bsz, s, 2 * D_RNN)
    y, h_last, conv_new = _lru_scan(
        proj, lru_h0.reshape(bsz, 1, D_RNN), lru_conv0, wts["conv_w"], row(wts["conv_b"]),
        wts["w_a"], row(wts["b_a"]), wts["w_x"], row(wts["b_x"]), row(wts["lam"]), ts)
    *routed, counts = _out_proj(y.reshape(t, D_RNN), wts["w_out"], x.reshape(t, d),
                                row(wts["ln_g"][0, 0]), row(wts["ln_b"][0, 0]), *wts["router"][0], tm, not sparse)
    x2 = moe(0, routed, counts)
    p0 = jnp.pad(pool0, ((0, 0), (1, 0), (0, 0)))
    *routed, pool_new, counts = _pool_mix(
        x2, bsz, s, 0, p0, wts["pool_w"], row(wts["pool_scale"]),
        row(wts["ln_g"][1, 0]), row(wts["ln_b"][1, 0]), *wts["router"][1], ts, pos0, not sparse)
    y_out = moe(1, routed, counts)
    return (y_out.reshape(bsz, s, d), h_last.reshape(1, bsz, D_RNN), conv_new[None],
            pool_new[None, :, 1:, :])


def kernel(x_prompt, x_sample, state_lru_h, state_lru_conv, state_pool, lru_w_in, lru_conv_w, lru_conv_b, lru_w_a, lru_b_a, lru_w_x, lru_b_x, lru_lambda, lru_w_out, pool_w, pool_scale, ln_g, ln_b, moe_router_g_w, moe_router_g_b, moe_router_e_w, moe_router_e_b, moe_w_gate_up, moe_w_down):
    wts = {
        "w_in": lru_w_in[0].astype(BF16), "conv_w": lru_conv_w[0], "conv_b": lru_conv_b[0],
        "w_a": lru_w_a[0].astype(BF16), "b_a": lru_b_a[0], "w_x": lru_w_x[0].astype(BF16), "b_x": lru_b_x[0],
        "lam": lru_lambda[0], "w_out": lru_w_out[0].astype(BF16),
        "pool_w": pool_w[0].astype(BF16), "pool_scale": pool_scale[0], "ln_g": ln_g, "ln_b": ln_b,
        "router": [_router_weights(moe_router_g_w[i], moe_router_g_b[i], moe_router_e_w[i], moe_router_e_b[i])
                   for i in range(DEPTH)],
        "wgu": moe_w_gate_up.astype(BF16), "wd": moe_w_down.astype(BF16),
    }
    bp = x_prompt.shape[0]
    dt = x_prompt.dtype
    y_p, h_p, c_p, p_p = _trunk(
        x_prompt, jnp.zeros((bp, D_RNN), dt), jnp.zeros((bp, CONV_WIDTH - 1, D_RNN), dt),
        jnp.zeros((bp, POOL_STATE, D_MODEL), dt), 0, wts, ts=256, tm=256, sparse=True)
    y_s, h_s, c_s, p_s = _trunk(
        x_sample, state_lru_h[0], state_lru_conv[0], state_pool[0], PAST_LEN, wts,
        ts=x_sample.shape[1], tm=256, sparse=False)
    return (y_p, y_s, h_p, c_p, p_p, h_s, c_s, p_s)
```

```python
import functools

import jax
import jax.numpy as jnp
from jax import lax
from jax.experimental import pallas as pl
from jax.experimental.pallas import tpu as pltpu

F32 = jnp.float32
BF16 = jnp.bfloat16

D_MODEL = 2048
DEPTH = 2
D_RNN = D_MODEL
LRU_HEADS = 8
LRU_BLOCK = D_RNN // LRU_HEADS
CONV_WIDTH = 4
LRU_C = 8.0
POOL_WINDOWS = (2, 4, 8, 16)
POOL_GROUPS = 4
POOL_GROUP_DIM = D_MODEL // POOL_GROUPS
POOL_STATE = max(POOL_WINDOWS) - 1
MOE_GROUPS = 4
MOE_EXPERTS_PER_GROUP = 4
N_EXPERTS = MOE_GROUPS * MOE_EXPERTS_PER_GROUP
D_EXPERT = 512
DEEPNORM_ALPHA = (2.0 * DEPTH) ** 0.25
LN_EPS = 1e-5
PAST_LEN = 2048

LANES = 128
SUBLANES = 8
ROUTER_LANES = LANES
ROW_WIDTH = D_MODEL + ROUTER_LANES
EXPERT_LANE0 = MOE_GROUPS
CLASS_LANE = 0
RANK_LANE = 1
GATE_LANE0 = 2
PAIRS_PER_GROUP = 6
N_CLASSES = MOE_GROUPS * PAIRS_PER_GROUP
PAIR_SLOTS = ((0, 1), (0, 2), (0, 3), (1, 3), (1, 2), (3, 2))
PAIR_ORDER = (0, 1, 2, 4, 3, 5)
VMEM_LIMIT = 56 * 1024 * 1024
MOE_TILE = 256


def _cparams(*sem):
    return pltpu.CompilerParams(dimension_semantics=sem, vmem_limit_bytes=VMEM_LIMIT)


def _layer_norm(z, g, b):
    mu = jnp.mean(z, axis=-1, keepdims=True)
    zc = z - mu
    var = jnp.mean(zc * zc, axis=-1, keepdims=True)
    return zc * lax.rsqrt(var + LN_EPS) * g + b


def _route(x1, wr_a_ref, wr_b_ref, rb_ref, cnt_ref, first):
    @pl.when(first)
    def _():
        cnt_ref[...] = jnp.zeros_like(cnt_ref)

    hi = x1.astype(BF16)
    lo = (x1 - hi.astype(F32)).astype(BF16)
    c = jnp.dot(hi, wr_a_ref[...], preferred_element_type=F32)
    d = jnp.dot(lo, wr_b_ref[...], preferred_element_type=F32)
    logits = c[:, :ROUTER_LANES] + c[:, ROUTER_LANES:] + d + rb_ref[...]
    tm = logits.shape[0]
    lane = lax.broadcasted_iota(jnp.int32, (tm, ROUTER_LANES), 1)
    neg = jnp.float32(-jnp.inf)
    is_g = lane < MOE_GROUPS
    lg = jnp.where(is_g, logits, neg)
    gmax = jnp.max(lg, axis=1, keepdims=True)
    gsel = jnp.min(jnp.where(lg == gmax, lane, ROUTER_LANES), axis=1, keepdims=True)
    p_sel = 1.0 / jnp.sum(jnp.where(is_g, jnp.exp(logits - gmax), 0.0), axis=1, keepdims=True)
    elane = lane - EXPERT_LANE0
    in_g = (elane >= 0) & (elane < N_EXPERTS) & ((elane >> 2) == gsel)
    le = jnp.where(in_g, logits, neg)
    t1 = jnp.max(le, axis=1, keepdims=True)
    i1 = jnp.min(jnp.where(le == t1, lane, ROUTER_LANES), axis=1, keepdims=True)
    le2 = jnp.where(lane == i1, neg, le)
    t2 = jnp.max(le2, axis=1, keepdims=True)
    i2 = jnp.min(jnp.where(le2 == t2, lane, ROUTER_LANES), axis=1, keepdims=True)
    e2 = jnp.exp(t2 - t1)
    inv = p_sel / (1.0 + e2)
    gates = jnp.where(lane == i1, inv, 0.0) + jnp.where(lane == i2, inv * e2, 0.0)
    base = EXPERT_LANE0 + MOE_EXPERTS_PER_GROUP * gsel
    pa = jnp.minimum(i1, i2) - base
    pb = jnp.maximum(i1, i2) - base
    cls = PAIRS_PER_GROUP * gsel + 3 * pa - ((pa * (pa - 1)) >> 1) + pb - pa - 1
    kp = cls - PAIRS_PER_GROUP * gsel
    kp = jnp.where(kp == 3, 4, jnp.where(kp == 4, 3, kp))
    pos = PAIRS_PER_GROUP * gsel + kp
    first_local = jnp.where(kp < 3, 0, jnp.where(kp < 5, 1, 3))
    i1_first = (i1 - base) == first_local
    g_first = jnp.where(i1_first, inv, inv * e2)
    g_second = jnp.where(i1_first, inv * e2, inv)
    onehot = lane == pos
    tri = (lax.broadcasted_iota(jnp.int32, (tm, tm), 0) >= lax.broadcasted_iota(jnp.int32, (tm, tm), 1))
    cum = jnp.dot(tri.astype(BF16), onehot.astype(BF16), preferred_element_type=F32)
    run = cnt_ref[0:1, :]
    rank = jnp.sum(jnp.where(onehot, cum + run, 0.0), axis=1, keepdims=True) - 1.0
    cnt_ref[...] = jnp.broadcast_to(run + cum[tm - 1:tm, :], cnt_ref.shape)
    for ln, val in ((CLASS_LANE, pos.astype(F32)), (RANK_LANE, rank), (GATE_LANE0, g_first), (GATE_LANE0 + 1, g_second)):
        gates = jnp.where(lane == ln, val, gates)
    return gates


def _lru_kernel(x_ref, win_ref, h0_ref, conv0_ref, cw_ref, cb_ref, wa_ref, ba_ref, wx_ref, bx_ref,
                lam_ref, y_ref, hl_ref, cn_ref, ubuf, gbuf, ucbuf, rbuf, ibuf, hc, *, ts):
    j = pl.program_id(1)

    @pl.when(j == 0)
    def _():
        ubuf[0:SUBLANES, :] = jnp.zeros((SUBLANES, D_RNN), F32)
        ubuf[SUBLANES - (CONV_WIDTH - 1):SUBLANES, :] = conv0_ref[...]
        hc[...] = jnp.broadcast_to(h0_ref[...], (SUBLANES, D_RNN))

    xb = x_ref[...].astype(BF16)
    half_sp = (-0.5 * LRU_C) * jax.nn.softplus(-lam_ref[...])
    row = lax.broadcasted_iota(jnp.int32, (SUBLANES, LANES), 0)
    for h in range(LRU_HEADS):
        hs = slice(h * LRU_BLOCK, (h + 1) * LRU_BLOCK)
        gbuf[:, hs] = jnp.dot(xb, win_ref[h], preferred_element_type=F32)
        ubuf[SUBLANES:SUBLANES + ts, hs] = jnp.dot(xb, win_ref[LRU_HEADS + h], preferred_element_type=F32)
        uc = cb_ref[:, hs] + ubuf[SUBLANES - 3:SUBLANES - 3 + ts, hs] * cw_ref[0:1, hs]
        uc = uc + ubuf[SUBLANES - 2:SUBLANES - 2 + ts, hs] * cw_ref[1:2, hs]
        uc = uc + ubuf[SUBLANES - 1:SUBLANES - 1 + ts, hs] * cw_ref[2:3, hs]
        uc = uc + ubuf[SUBLANES:SUBLANES + ts, hs] * cw_ref[3:4, hs]
        ucbuf[:, hs] = uc
        ucb = uc.astype(BF16)
        rbuf[:, hs] = jnp.dot(ucb, wa_ref[h], preferred_element_type=F32) + ba_ref[:, hs]
        ibuf[:, hs] = jnp.dot(ucb, wx_ref[h], preferred_element_type=F32) + bx_ref[:, hs]

        blocks = [slice(c * LANES, (c + 1) * LANES) for c in range(h * LRU_BLOCK // LANES, (h + 1) * LRU_BLOCK // LANES)]
        hprev = [hc[:, cs] for cs in blocks]
        hsp = [jnp.broadcast_to(half_sp[:, cs], (SUBLANES, LANES)) for cs in blocks]
        for p in range(ts // (2 * SUBLANES)):
            for k, cs in enumerate(blocks):
                pair = []
                for q in range(2):
                    rows = slice((2 * p + q) * SUBLANES, (2 * p + q + 1) * SUBLANES)
                    log_a = hsp[k] * jnp.tanh(rbuf[rows, cs]) + hsp[k]
                    a = jnp.exp(log_a)
                    mult = jnp.exp(0.5 * jnp.log(-jnp.tanh(log_a) * (a * a + 1.0)))
                    uch = ucbuf[rows, cs]
                    b = mult * (jnp.tanh(ibuf[rows, cs]) * uch + uch)
                    b = jnp.where(row == 0, a * pltpu.roll(hprev[k], 1, 0) + b, b)
                    for s in (1, 2, 4):
                        keep = row >= s
                        b = jnp.where(keep, a * pltpu.roll(b, s, 0) + b, b)
                        if s < 4:
                            a = jnp.where(keep, a * pltpu.roll(a, s, 0), a)
                    hprev[k] = b
                    pair.append(b)
                rows2 = slice(2 * p * SUBLANES, (2 * p + 2) * SUBLANES)
                y_ref[rows2, cs] = (jnp.concatenate(pair, axis=0) * jax.nn.gelu(gbuf[rows2, cs])).astype(BF16)
        for k, cs in enumerate(blocks):
            hc[:, cs] = hprev[k]
    cn_ref[...] = ubuf[SUBLANES + ts - (CONV_WIDTH - 1):SUBLANES + ts, :]
    ubuf[0:SUBLANES, :] = ubuf[ts:ts + SUBLANES, :]
    hl_ref[...] = hc[SUBLANES - 1:SUBLANES, :]


def _lru_mix(x, w_in, h0, conv0, cw, cb, wa, ba, wx, bx, lam, ts):
    bsz, s, _ = x.shape
    row2 = lambda b, j: (0, 0)
    tile_f32 = pltpu.VMEM((ts, D_RNN), F32)
    return pl.pallas_call(
        functools.partial(_lru_kernel, ts=ts),
        grid=(bsz, s // ts),
        in_specs=[pl.BlockSpec((None, ts, D_MODEL), lambda b, j: (b, j, 0)),
                  pl.BlockSpec((2 * LRU_HEADS, D_MODEL, LRU_BLOCK), lambda b, j: (0, 0, 0),
                               pipeline_mode=pl.Buffered(1)),
                  pl.BlockSpec((None, 1, D_RNN), lambda b, j: (b, 0, 0)),
                  pl.BlockSpec((None, CONV_WIDTH - 1, D_RNN), lambda b, j: (b, 0, 0)),
                  pl.BlockSpec((CONV_WIDTH, D_RNN), row2),
                  pl.BlockSpec((1, D_RNN), row2),
                  pl.BlockSpec((LRU_HEADS, LRU_BLOCK, LRU_BLOCK), lambda b, j: (0, 0, 0)),
                  pl.BlockSpec((1, D_RNN), row2),
                  pl.BlockSpec((LRU_HEADS, LRU_BLOCK, LRU_BLOCK), lambda b, j: (0, 0, 0)),
                  pl.BlockSpec((1, D_RNN), row2),
                  pl.BlockSpec((1, D_RNN), row2)],
        out_specs=[pl.BlockSpec((None, ts, D_RNN), lambda b, j: (b, j, 0)),
                   pl.BlockSpec((None, 1, D_RNN), lambda b, j: (b, 0, 0)),
                   pl.BlockSpec((None, CONV_WIDTH - 1, D_RNN), lambda b, j: (b, 0, 0))],
        out_shape=[jax.ShapeDtypeStruct((bsz, s, D_RNN), BF16),
                   jax.ShapeDtypeStruct((bsz, 1, D_RNN), F32),
                   jax.ShapeDtypeStruct((bsz, CONV_WIDTH - 1, D_RNN), F32)],
        scratch_shapes=[pltpu.VMEM((SUBLANES + ts, D_RNN), F32), tile_f32, tile_f32, tile_f32, tile_f32,
                        pltpu.VMEM((SUBLANES, D_RNN), F32)],
        compiler_params=_cparams("arbitrary", "arbitrary"),
        name="lru_mix",
    )(x, w_in, h0, conv0, cw, cb, wa, ba, wx, bx, lam)


def _emit_routed(x1, route, outs, dense_out):
    if dense_out:
        x1e_ref, x1b_ref, gates_ref = outs
        x1b_ref[...] = x1.astype(BF16)
        gates_ref[...] = route
    else:
        x1e_ref, meta_ref = outs
        meta_ref[...] = route.T[0:SUBLANES, :]
    x1e_ref[:, :D_MODEL] = x1
    x1e_ref[:, D_MODEL:] = route


def _out_kernel(y_ref, w_ref, x_ref, g_ref, b_ref, wr_a_ref, wr_b_ref, rb_ref, *outs, dense_out):
    *outs, cnt_ref = outs
    mix = jnp.dot(y_ref[...], w_ref[...], preferred_element_type=F32)
    x1 = _layer_norm(DEEPNORM_ALPHA * x_ref[...] + mix, g_ref[...], b_ref[...])
    route = _route(x1, wr_a_ref, wr_b_ref, rb_ref, cnt_ref, pl.program_id(0) == 0)
    _emit_routed(x1, route, outs, dense_out)


def _routed_out_specs(lead, rows, tile, dense_out):
    blk = (None,) * (len(lead) - 1)
    n_rows = lead[-1]
    specs = [pl.BlockSpec(blk + (rows, ROW_WIDTH), tile)]
    shapes = [jax.ShapeDtypeStruct(lead + (ROW_WIDTH,), F32)]
    if dense_out:
        specs += [pl.BlockSpec(blk + (rows, D_MODEL), tile), pl.BlockSpec(blk + (rows, ROUTER_LANES), tile)]
        shapes += [jax.ShapeDtypeStruct(lead + (D_MODEL,), BF16), jax.ShapeDtypeStruct(lead + (ROUTER_LANES,), F32)]
    else:
        specs += [pl.BlockSpec(blk + (None, SUBLANES, rows), lambda *i: tile(*i) + (0,))]
        shapes += [jax.ShapeDtypeStruct(lead[:-1] + (n_rows // rows, SUBLANES, rows), F32)]
    return specs, shapes


def _out_proj(y, w, x, g, b, wr_a, wr_b, rb, tm, dense_out):
    t = x.shape[0]
    full = lambda i: (0, 0)
    tile = lambda i: (i, 0)
    specs, shapes = _routed_out_specs((t,), tm, tile, dense_out)
    return pl.pallas_call(
        functools.partial(_out_kernel, dense_out=dense_out),
        grid=(t // tm,),
        in_specs=[pl.BlockSpec((tm, D_RNN), tile),
                  pl.BlockSpec((D_RNN, D_MODEL), full),
                  pl.BlockSpec((tm, D_MODEL), tile),
                  pl.BlockSpec((1, D_MODEL), full),
                  pl.BlockSpec((1, D_MODEL), full),
                  pl.BlockSpec((D_MODEL, 2 * ROUTER_LANES), full),
                  pl.BlockSpec((D_MODEL, ROUTER_LANES), full),
                  pl.BlockSpec((1, ROUTER_LANES), full)],
        out_specs=specs + [pl.BlockSpec((SUBLANES, ROUTER_LANES), full)],
        out_shape=shapes + [jax.ShapeDtypeStruct((SUBLANES, ROUTER_LANES), F32)],
        compiler_params=_cparams("arbitrary"),
        name="lru_out_ln_route",
    )(y, w, x, g, b, wr_a, wr_b, rb)


def _pool_kernel(x_ref, p0_ref, pw_ref, sc_ref, g_ref, b_ref, wr_a_ref, wr_b_ref, rb_ref,
                 *outs, ts, pos0, dense_out):
    *outs, pn_ref, cnt_ref, ext = outs
    j = pl.program_id(1)
    hist = POOL_STATE + 1

    @pl.when(j == 0)
    def _():
        ext[0:hist, :] = p0_ref[...]

    x = x_ref[...]
    ext[hist:hist + ts, :] = x
    pos = (pos0 + j * ts + lax.broadcasted_iota(jnp.int32, (ts, 1), 0)).astype(F32)
    mixes = []
    for g, w in enumerate(POOL_WINDOWS):
        lo, hi = g * POOL_GROUP_DIM, (g + 1) * POOL_GROUP_DIM
        acc = x[:, lo:hi]
        for k in range(1, w):
            acc = acc + ext[hist - k:hist - k + ts, lo:hi]
        mean = acc / jnp.minimum(jnp.float32(w), pos + 1.0)
        diff = (mean - x[:, lo:hi]).astype(BF16)
        mixes.append(jnp.dot(diff, pw_ref[g], preferred_element_type=F32))
    mix = jnp.concatenate(mixes, axis=1) * sc_ref[...]
    pn_ref[...] = ext[ts:ts + hist, :]
    ext[0:hist, :] = ext[ts:ts + hist, :]
    x1 = _layer_norm(DEEPNORM_ALPHA * x + mix, g_ref[...], b_ref[...])
    first = (pl.program_id(0) == 0) & (j == 0)
    route = _route(x1, wr_a_ref, wr_b_ref, rb_ref, cnt_ref, first)
    _emit_routed(x1, route, outs, dense_out)


def _pool_mix(x, bsz, s, row0, p0, pw, sc, g, b, wr_a, wr_b, rb, ts, pos0, dense_out):
    hist = POOL_STATE + 1
    full = lambda bb, j: (0, 0)
    tile = lambda bb, j: (bb, j, 0)
    specs, shapes = _routed_out_specs((bsz, s), ts, tile, dense_out)
    assert row0 % ts == 0 and s % ts == 0
    return pl.pallas_call(
        functools.partial(_pool_kernel, ts=ts, pos0=pos0, dense_out=dense_out),
        grid=(bsz, s // ts),
        in_specs=[pl.BlockSpec((ts, D_MODEL), lambda bb, j: (row0 // ts + bb * (s // ts) + j, 0)),
                  pl.BlockSpec((None, hist, D_MODEL), lambda bb, j: (bb, 0, 0)),
                  pl.BlockSpec((POOL_GROUPS, POOL_GROUP_DIM, POOL_GROUP_DIM), lambda bb, j: (0, 0, 0)),
                  pl.BlockSpec((1, D_MODEL), full),
                  pl.BlockSpec((1, D_MODEL), full),
                  pl.BlockSpec((1, D_MODEL), full),
                  pl.BlockSpec((D_MODEL, 2 * ROUTER_LANES), full),
                  pl.BlockSpec((D_MODEL, ROUTER_LANES), full),
                  pl.BlockSpec((1, ROUTER_LANES), full)],
        out_specs=specs + [pl.BlockSpec((None, hist, D_MODEL), lambda bb, j: (bb, 0, 0)),
                           pl.BlockSpec((SUBLANES, ROUTER_LANES), full)],
        out_shape=shapes + [jax.ShapeDtypeStruct((bsz, hist, D_MODEL), F32),
                            jax.ShapeDtypeStruct((SUBLANES, ROUTER_LANES), F32)],
        scratch_shapes=[pltpu.VMEM((hist + ts, D_MODEL), F32)],
        compiler_params=_cparams("arbitrary", "arbitrary"),
        name="pool_mix_ln_route",
    )(x, p0, pw, sc, g, b, wr_a, wr_b, rb)


def _moe_kernel(xb_ref, gates_ref, wgu_ref, wd_ref, x1_ref, g_ref, b_ref, o_ref, acc):
    e = pl.program_id(1)

    @pl.when(e == 0)
    def _():
        acc[...] = jnp.zeros_like(acc)

    hu = jnp.dot(xb_ref[...], wgu_ref[...], preferred_element_type=F32)
    hg, hv = hu[:, :D_EXPERT], hu[:, D_EXPERT:]
    gates = gates_ref[...]
    lane = lax.broadcasted_iota(jnp.int32, gates.shape, 1)
    gate = jnp.sum(jnp.where(lane == e + EXPERT_LANE0, gates, 0.0), axis=1, keepdims=True)
    act = (jax.nn.silu(hg) * hv * gate).astype(BF16)
    acc[...] += jnp.dot(act, wd_ref[...], preferred_element_type=F32)

    @pl.when(e == N_EXPERTS - 1)
    def _():
        o_ref[...] = _layer_norm(DEEPNORM_ALPHA * x1_ref[...] + acc[...], g_ref[...], b_ref[...])


def _moe(xb, gates, wgu, wd, layer, x1, g, b, tm):
    t = xb.shape[0]
    tile = lambda i, e: (i, 0)
    full = lambda i, e: (0, 0)
    return pl.pallas_call(
        _moe_kernel,
        grid=(t // tm, N_EXPERTS),
        in_specs=[pl.BlockSpec((tm, D_MODEL), tile),
                  pl.BlockSpec((tm, ROUTER_LANES), tile),
                  pl.BlockSpec((None, None, D_MODEL, 2 * D_EXPERT), lambda i, e: (layer, e, 0, 0)),
                  pl.BlockSpec((None, None, D_EXPERT, D_MODEL), lambda i, e: (layer, e, 0, 0)),
                  pl.BlockSpec((tm, D_MODEL), tile),
                  pl.BlockSpec((1, D_MODEL), full),
                  pl.BlockSpec((1, D_MODEL), full)],
        out_specs=pl.BlockSpec((tm, D_MODEL), tile),
        out_shape=jax.ShapeDtypeStruct((t, D_MODEL), F32),
        scratch_shapes=[pltpu.VMEM((tm, D_MODEL), F32)],
        compiler_params=_cparams("arbitrary", "arbitrary"),
        name="moe_ffn_ln",
    )(xb, gates, wgu, wd, x1, g, b)


def _dispatch(meta, counts, tm):
    t = meta.shape[0] * meta.shape[-1]
    n_tiles = t // tm
    n_steps_max = n_tiles + N_CLASSES - 1
    cpos = jnp.arange(N_CLASSES, dtype=jnp.int32)
    slots = jnp.array(PAIR_SLOTS, jnp.int32)
    exp_ab = MOE_EXPERTS_PER_GROUP * (cpos // PAIRS_PER_GROUP)[:, None] + slots[cpos % PAIRS_PER_GROUP]
    pos = meta[:, CLASS_LANE, :].reshape(t).astype(jnp.int32)
    rank = meta[:, RANK_LANE, :].reshape(t).astype(jnp.int32)
    counts = counts[0, :N_CLASSES].astype(jnp.int32)
    cend = jnp.cumsum(counts)
    cstart = cend - counts
    dest = cstart[pos] + rank
    src = jnp.zeros((t,), jnp.int32).at[dest].set(jnp.arange(t, dtype=jnp.int32))
    span = jnp.where(counts > 0, (cend - 1) // tm - cstart // tm + 1, 0)
    send = jnp.cumsum(span)
    sstart = send - span
    n_steps = send[-1]
    step = jnp.arange(n_steps_max, dtype=jnp.int32)
    live = step < n_steps
    cls = jnp.sum((step[:, None] >= send[None, :]).astype(jnp.int32), axis=1)
    cls = jnp.where(live, cls, cls[n_steps - 1])
    tile = jnp.where(live, cstart[cls] // tm + step - sstart[cls], n_tiles - 1)
    prev_tile = jnp.concatenate([jnp.full((1,), -1, jnp.int32), tile[:-1]])
    next_tile = jnp.concatenate([tile[1:], jnp.full((1,), -1, jnp.int32)])
    first = live & (tile != prev_tile)
    last = live & ((tile != next_tile) | (step == n_steps - 1))
    i32 = lambda v: v.astype(jnp.int32)
    return (src.reshape(n_tiles, 1, tm), i32(tile), exp_ab[cls, 0], exp_ab[cls, 1], i32(cls),
            i32(first), i32(last), i32(n_steps).reshape(1))


def _moe_sparse_kernel(tile_ref, ea_ref, eb_ref, cls_ref, first_ref, last_ref, nst_ref,
                       src_ref, srcn_ref, x1_hbm, wgu_a, wgu_b, wd_a, wd_b, g_ref, b_ref,
                       x2_hbm, xbuf, obuf, acc, gsem, ssem, *, tm, n_tiles):
    del ea_ref, eb_ref
    s = pl.program_id(0)
    live = s < nst_ref[0]
    tau = tile_ref[s]
    slot = tau % 2
    first = first_ref[s] == 1
    last = last_ref[s] == 1

    def start_gather(idx_ref, sl):
        for r in range(tm):
            pltpu.make_async_copy(x1_hbm.at[pl.ds(idx_ref[0, r], 1)], xbuf.at[sl, pl.ds(r, 1)], gsem.at[sl]).start()

    def wait_gather(sl):
        pltpu.make_async_copy(x1_hbm.at[pl.ds(0, tm)], xbuf.at[sl], gsem.at[sl]).wait()

    def wait_scatter(sl):
        pltpu.make_async_copy(obuf.at[sl], x2_hbm.at[pl.ds(0, tm)], ssem.at[sl]).wait()

    @pl.when(s == 0)
    def _():
        acc[...] = jnp.zeros_like(acc)
        start_gather(src_ref, 0)

    @pl.when(live & first)
    def _():
        wait_gather(slot)

        @pl.when(tau + 1 < n_tiles)
        def _():
            start_gather(srcn_ref, 1 - slot)

    @pl.when(live)
    def _():
        xb = xbuf[slot, :, :D_MODEL].astype(BF16)
        route = xbuf[slot, :, D_MODEL:]
        mine = route[:, CLASS_LANE:CLASS_LANE + 1] == cls_ref[s].astype(F32)
        out = None
        for col, (wgu, wd) in enumerate(((wgu_a, wd_a), (wgu_b, wd_b))):
            gate = jnp.where(mine, route[:, GATE_LANE0 + col:GATE_LANE0 + col + 1], 0.0)
            hu = jnp.dot(xb, wgu[...], preferred_element_type=F32)
            act = (jax.nn.silu(hu[:, :D_EXPERT]) * hu[:, D_EXPERT:] * gate).astype(BF16)
            o = jnp.dot(act, wd[...], preferred_element_type=F32)
            out = o if out is None else out + o

        acc[...] = jnp.where(first, out, acc[...] + out)

    @pl.when(live & last)
    def _():
        @pl.when(tau >= 2)
        def _():
            wait_scatter(slot)

        obuf[slot] = _layer_norm(DEEPNORM_ALPHA * xbuf[slot, :, :D_MODEL] + acc[...], g_ref[...], b_ref[...])
        for r in range(tm):
            pltpu.make_async_copy(obuf.at[slot, pl.ds(r, 1)], x2_hbm.at[pl.ds(src_ref[0, r], 1)], ssem.at[slot]).start()

    @pl.when(s == nst_ref[0] - 1)
    def _():
        wait_scatter(slot)

        @pl.when(tau >= 1)
        def _():
            wait_scatter(1 - slot)


def _moe_sparse(x1e, meta, counts, wgu, wd, layer, g, b, tm):
    t = x1e.shape[0]
    src, tile, exp_a, exp_b, cls, first, last, n_steps = _dispatch(meta, counts, tm)
    n_tiles = t // tm
    full = lambda s, *_: (0, 0)
    grid_spec = pltpu.PrefetchScalarGridSpec(
        num_scalar_prefetch=7,
        grid=(tile.shape[0],),
        in_specs=[pl.BlockSpec((None, 1, tm), lambda s, tl, *_: (tl[s], 0, 0), memory_space=pltpu.SMEM),
                  pl.BlockSpec((None, 1, tm), lambda s, tl, *_: (jnp.minimum(tl[s] + 1, n_tiles - 1), 0, 0),
                               memory_space=pltpu.SMEM),
                  pl.BlockSpec(memory_space=pl.ANY),
                  pl.BlockSpec((None, None, D_MODEL, 2 * D_EXPERT), lambda s, tl, ea, eb, *_: (layer, ea[s], 0, 0)),
                  pl.BlockSpec((None, None, D_MODEL, 2 * D_EXPERT), lambda s, tl, ea, eb, *_: (layer, eb[s], 0, 0)),
                  pl.BlockSpec((None, None, D_EXPERT, D_MODEL), lambda s, tl, ea, eb, *_: (layer, ea[s], 0, 0)),
                  pl.BlockSpec((None, None, D_EXPERT, D_MODEL), lambda s, tl, ea, eb, *_: (layer, eb[s], 0, 0)),
                  pl.BlockSpec((1, D_MODEL), full),
                  pl.BlockSpec((1, D_MODEL), full)],
        out_specs=pl.BlockSpec(memory_space=pl.ANY),
        scratch_shapes=[pltpu.VMEM((2, tm, ROW_WIDTH), F32), pltpu.VMEM((2, tm, D_MODEL), F32),
                        pltpu.VMEM((tm, D_MODEL), F32),
                        pltpu.SemaphoreType.DMA((2,)), pltpu.SemaphoreType.DMA((2,))],
    )
    return pl.pallas_call(
        functools.partial(_moe_sparse_kernel, tm=tm, n_tiles=n_tiles),
        grid_spec=grid_spec,
        out_shape=jax.ShapeDtypeStruct((t, D_MODEL), F32),
        compiler_params=_cparams("arbitrary"),
        name="moe_sparse_ln",
    )(tile, exp_a, exp_b, cls, first, last, n_steps, src, src, x1e, wgu, wgu, wd, wd, g, b)


def _router_weights(wg, bg, we, be):
    w = jnp.concatenate([wg, we], axis=1)
    w = jnp.pad(w, ((0, 0), (0, ROUTER_LANES - w.shape[1])))
    w_hi = w.astype(BF16)
    w_lo = (w - w_hi.astype(F32)).astype(BF16)
    rb = jnp.pad(jnp.concatenate([bg, be]), (0, ROUTER_LANES - MOE_GROUPS - N_EXPERTS))
    return jnp.concatenate([w_hi, w_lo], axis=1), w_hi, rb.reshape(1, ROUTER_LANES)


def _trunk(x, lru_h0, lru_conv0, pool0, pos0, wts, ts, tm, sparse):
    bsz, s, d = x.shape
    t = bsz * s
    row = lambda v: v.reshape(1, -1)

    def moe(layer, routed, counts):
        g, b = row(wts["ln_g"][layer, 1]), row(wts["ln_b"][layer, 1])
        if sparse:
            x1e, meta = routed
            meta = meta.reshape(-1, SUBLANES, meta.shape[-1])
            return _moe_sparse(x1e.reshape(t, ROW_WIDTH), meta, counts, wts["wgu"], wts["wd"], layer, g, b, MOE_TILE)
        x1e, x1b, gates = (v.reshape(t, -1) for v in routed)
        return _moe(x1b, gates, wts["wgu"], wts["wd"], layer, x1e[:, :D_MODEL], g, b, min(512, t))

    y, h_last, conv_new = _lru_mix(
        x, wts["w_in"], lru_h0.reshape(bsz, 1, D_RNN), lru_conv0, 0.5 * wts["conv_w"], row(0.5 * wts["conv_b"]),
        wts["w_a"], row(0.5 * wts["b_a"]), wts["w_x"], row(0.5 * wts["b_x"]), row(wts["lam"]), ts)
    *routed, counts = _out_proj(y.reshape(t, D_RNN), wts["w_out"], x.reshape(t, d),
                                row(wts["ln_g"][0, 0]), row(wts["ln_b"][0, 0]), *wts["router"][0], tm, not sparse)
    x2 = moe(0, routed, counts)
    p0 = jnp.pad(pool0, ((0, 0), (1, 0), (0, 0)))
    *routed, pool_new, counts = _pool_mix(
        x2, bsz, s, 0, p0, wts["pool_w"], row(wts["pool_scale"]),
        row(wts["ln_g"][1, 0]), row(wts["ln_b"][1, 0]), *wts["router"][1], ts, pos0, not sparse)
    y_out = moe(1, routed, counts)
    return (y_out.reshape(bsz, s, d), h_last.reshape(1, bsz, D_RNN), conv_new[None],
            pool_new[None, :, 1:, :])


def kernel(x_prompt, x_sample, state_lru_h, state_lru_conv, state_pool, lru_w_in, lru_conv_w, lru_conv_b, lru_w_a, lru_b_a, lru_w_x, lru_b_x, lru_lambda, lru_w_out, pool_w, pool_scale, ln_g, ln_b, moe_router_g_w, moe_router_g_b, moe_router_e_w, moe_router_e_b, moe_w_gate_up, moe_w_down):
    wts = {
        "w_in": lru_w_in[0].astype(BF16).reshape(D_MODEL, 2 * LRU_HEADS, LRU_BLOCK).transpose(1, 0, 2), "conv_w": lru_conv_w[0], "conv_b": lru_conv_b[0],
        "w_a": lru_w_a[0].astype(BF16), "b_a": lru_b_a[0], "w_x": lru_w_x[0].astype(BF16), "b_x": lru_b_x[0],
        "lam": lru_lambda[0], "w_out": lru_w_out[0].astype(BF16),
        "pool_w": pool_w[0].astype(BF16), "pool_scale": pool_scale[0], "ln_g": ln_g, "ln_b": ln_b,
        "router": [_router_weights(moe_router_g_w[i], moe_router_g_b[i], moe_router_e_w[i], moe_router_e_b[i])
                   for i in range(DEPTH)],
        "wgu": moe_w_gate_up.astype(BF16), "wd": moe_w_down.astype(BF16),
    }
    bp = x_prompt.shape[0]
    dt = x_prompt.dtype
    y_p, h_p, c_p, p_p = _trunk(
        x_prompt, jnp.zeros((bp, D_RNN), dt), jnp.zeros((bp, CONV_WIDTH - 1, D_RNN), dt),
        jnp.zeros((bp, POOL_STATE, D_MODEL), dt), 0, wts, ts=256, tm=256, sparse=True)
    y_s, h_s, c_s, p_s = _trunk(
        x_sample, state_lru_h[0], state_lru_conv[0], state_pool[0], PAST_LEN, wts,
        ts=x_sample.shape[1], tm=256, sparse=False)
    return (y_p, y_s, h_p, c_p, p_p, h_s, c_s, p_s)
```

```python
import functools

import jax
import jax.numpy as jnp
from jax import lax
from jax.experimental import pallas as pl
from jax.experimental.pallas import tpu as pltpu

F32 = jnp.float32
BF16 = jnp.bfloat16

D_MODEL = 2048
DEPTH = 2
D_RNN = D_MODEL
LRU_HEADS = 8
LRU_BLOCK = D_RNN // LRU_HEADS
CONV_WIDTH = 4
LRU_C = 8.0
POOL_WINDOWS = (2, 4, 8, 16)
POOL_GROUPS = 4
POOL_GROUP_DIM = D_MODEL // POOL_GROUPS
POOL_STATE = max(POOL_WINDOWS) - 1
MOE_GROUPS = 4
MOE_EXPERTS_PER_GROUP = 4
N_EXPERTS = MOE_GROUPS * MOE_EXPERTS_PER_GROUP
D_EXPERT = 512
DEEPNORM_ALPHA = (2.0 * DEPTH) ** 0.25
LN_EPS = 1e-5
PAST_LEN = 2048

LANES = 128
SUBLANES = 8
ROUTER_LANES = LANES
ROW_WIDTH = D_MODEL + ROUTER_LANES
EXPERT_LANE0 = MOE_GROUPS
CLASS_LANE = 0
RANK_LANE = 1
GATE_LANE0 = 2
PAIRS_PER_GROUP = 6
N_CLASSES = MOE_GROUPS * PAIRS_PER_GROUP
PAIR_SLOTS = ((0, 1), (0, 2), (0, 3), (1, 3), (1, 2), (3, 2))
PAIR_ORDER = (0, 1, 2, 4, 3, 5)
VMEM_LIMIT = 56 * 1024 * 1024
MOE_TILE = 256
ROUTE_ROWS = 256


def _cparams(*sem):
    return pltpu.CompilerParams(dimension_semantics=sem, vmem_limit_bytes=VMEM_LIMIT)


def _layer_norm(z, g, b):
    mu = jnp.mean(z, axis=-1, keepdims=True)
    zc = z - mu
    var = jnp.mean(zc * zc, axis=-1, keepdims=True)
    return zc * lax.rsqrt(var + LN_EPS) * g + b


def _route(x1, wr_a_ref, wr_b_ref, rb_ref, cnt_ref):
    hi = x1.astype(BF16)
    lo = (x1 - hi.astype(F32)).astype(BF16)
    c = jnp.dot(hi, wr_a_ref[...], preferred_element_type=F32)
    d = jnp.dot(lo, wr_b_ref[...], preferred_element_type=F32)
    logits = c[:, :ROUTER_LANES] + c[:, ROUTER_LANES:] + d + rb_ref[...]
    tm = logits.shape[0]
    lane = lax.broadcasted_iota(jnp.int32, (tm, ROUTER_LANES), 1)
    neg = jnp.float32(-jnp.inf)
    is_g = lane < MOE_GROUPS
    lg = jnp.where(is_g, logits, neg)
    gmax = jnp.max(lg, axis=1, keepdims=True)
    gsel = jnp.min(jnp.where(lg == gmax, lane, ROUTER_LANES), axis=1, keepdims=True)
    p_sel = 1.0 / jnp.sum(jnp.where(is_g, jnp.exp(logits - gmax), 0.0), axis=1, keepdims=True)
    elane = lane - EXPERT_LANE0
    in_g = (elane >= 0) & (elane < N_EXPERTS) & ((elane >> 2) == gsel)
    le = jnp.where(in_g, logits, neg)
    t1 = jnp.max(le, axis=1, keepdims=True)
    i1 = jnp.min(jnp.where(le == t1, lane, ROUTER_LANES), axis=1, keepdims=True)
    le2 = jnp.where(lane == i1, neg, le)
    t2 = jnp.max(le2, axis=1, keepdims=True)
    i2 = jnp.min(jnp.where(le2 == t2, lane, ROUTER_LANES), axis=1, keepdims=True)
    e2 = jnp.exp(t2 - t1)
    inv = p_sel / (1.0 + e2)
    gates = jnp.where(lane == i1, inv, 0.0) + jnp.where(lane == i2, inv * e2, 0.0)
    base = EXPERT_LANE0 + MOE_EXPERTS_PER_GROUP * gsel
    pa = jnp.minimum(i1, i2) - base
    pb = jnp.maximum(i1, i2) - base
    cls = PAIRS_PER_GROUP * gsel + 3 * pa - ((pa * (pa - 1)) >> 1) + pb - pa - 1
    kp = cls - PAIRS_PER_GROUP * gsel
    kp = jnp.where(kp == 3, 4, jnp.where(kp == 4, 3, kp))
    pos = PAIRS_PER_GROUP * gsel + kp
    first_local = jnp.where(kp < 3, 0, jnp.where(kp < 5, 1, 3))
    i1_first = (i1 - base) == first_local
    g_first = jnp.where(i1_first, inv, inv * e2)
    g_second = jnp.where(i1_first, inv * e2, inv)
    onehot = lane == pos
    tri = (lax.broadcasted_iota(jnp.int32, (tm, tm), 0) >= lax.broadcasted_iota(jnp.int32, (tm, tm), 1))
    cum = jnp.dot(tri.astype(BF16), onehot.astype(BF16), preferred_element_type=F32)
    run = cnt_ref[0:1, :]
    rank = jnp.sum(jnp.where(onehot, cum + run, 0.0), axis=1, keepdims=True) - 1.0
    cnt_ref[...] = jnp.broadcast_to(run + cum[tm - 1:tm, :], cnt_ref.shape)
    for ln, val in ((CLASS_LANE, pos.astype(F32)), (RANK_LANE, rank), (GATE_LANE0, g_first), (GATE_LANE0 + 1, g_second)):
        gates = jnp.where(lane == ln, val, gates)
    return gates


def _lru_kernel(x_ref, win_ref, h0_ref, conv0_ref, cw_ref, cb_ref, wa_ref, ba_ref, wx_ref, bx_ref,
                lam_ref, y_ref, hl_ref, cn_ref, ubuf, gbuf, ucbuf, rbuf, ibuf, hc, *, ts):
    j = pl.program_id(1)

    @pl.when(j == 0)
    def _():
        ubuf[0:SUBLANES, :] = jnp.zeros((SUBLANES, D_RNN), F32)
        ubuf[SUBLANES - (CONV_WIDTH - 1):SUBLANES, :] = conv0_ref[...]
        hc[...] = jnp.broadcast_to(h0_ref[...], (SUBLANES, D_RNN))

    xb = x_ref[...].astype(BF16)
    half_sp = (-0.5 * LRU_C) * jax.nn.softplus(-lam_ref[...])
    row = lax.broadcasted_iota(jnp.int32, (SUBLANES, LANES), 0)
    for h in range(LRU_HEADS):
        hs = slice(h * LRU_BLOCK, (h + 1) * LRU_BLOCK)
        gbuf[:, hs] = jnp.dot(xb, win_ref[h], preferred_element_type=F32)
        ubuf[SUBLANES:SUBLANES + ts, hs] = jnp.dot(xb, win_ref[LRU_HEADS + h], preferred_element_type=F32)
        uc = cb_ref[:, hs] + ubuf[SUBLANES - 3:SUBLANES - 3 + ts, hs] * cw_ref[0:1, hs]
        uc = uc + ubuf[SUBLANES - 2:SUBLANES - 2 + ts, hs] * cw_ref[1:2, hs]
        uc = uc + ubuf[SUBLANES - 1:SUBLANES - 1 + ts, hs] * cw_ref[2:3, hs]
        uc = uc + ubuf[SUBLANES:SUBLANES + ts, hs] * cw_ref[3:4, hs]
        ucbuf[:, hs] = uc
        ucb = uc.astype(BF16)
        rbuf[:, hs] = jnp.dot(ucb, wa_ref[h], preferred_element_type=F32) + ba_ref[:, hs]
        ibuf[:, hs] = jnp.dot(ucb, wx_ref[h], preferred_element_type=F32) + bx_ref[:, hs]

        blocks = [slice(c * LANES, (c + 1) * LANES) for c in range(h * LRU_BLOCK // LANES, (h + 1) * LRU_BLOCK // LANES)]
        hprev = [hc[:, cs] for cs in blocks]
        hsp = [jnp.broadcast_to(half_sp[:, cs], (SUBLANES, LANES)) for cs in blocks]
        for p in range(ts // (2 * SUBLANES)):
            for k, cs in enumerate(blocks):
                pair = []
                for q in range(2):
                    rows = slice((2 * p + q) * SUBLANES, (2 * p + q + 1) * SUBLANES)
                    log_a = hsp[k] * jnp.tanh(rbuf[rows, cs]) + hsp[k]
                    a = jnp.exp(log_a)
                    mult = jnp.exp(0.5 * jnp.log(-jnp.tanh(log_a) * (a * a + 1.0)))
                    uch = ucbuf[rows, cs]
                    b = mult * (jnp.tanh(ibuf[rows, cs]) * uch + uch)
                    b = jnp.where(row == 0, a * pltpu.roll(hprev[k], 1, 0) + b, b)
                    for s in (1, 2, 4):
                        keep = row >= s
                        b = jnp.where(keep, a * pltpu.roll(b, s, 0) + b, b)
                        if s < 4:
                            a = jnp.where(keep, a * pltpu.roll(a, s, 0), a)
                    hprev[k] = b
                    pair.append(b)
                rows2 = slice(2 * p * SUBLANES, (2 * p + 2) * SUBLANES)
                y_ref[rows2, cs] = (jnp.concatenate(pair, axis=0) * jax.nn.gelu(gbuf[rows2, cs])).astype(BF16)
        for k, cs in enumerate(blocks):
            hc[:, cs] = hprev[k]
    cn_ref[...] = ubuf[SUBLANES + ts - (CONV_WIDTH - 1):SUBLANES + ts, :]
    ubuf[0:SUBLANES, :] = ubuf[ts:ts + SUBLANES, :]
    hl_ref[...] = hc[SUBLANES - 1:SUBLANES, :]


def _lru_mix(x, w_in, h0, conv0, cw, cb, wa, ba, wx, bx, lam, ts):
    bsz, s, _ = x.shape
    row2 = lambda b, j: (0, 0)
    tile_f32 = pltpu.VMEM((ts, D_RNN), F32)
    return pl.pallas_call(
        functools.partial(_lru_kernel, ts=ts),
        grid=(bsz, s // ts),
        in_specs=[pl.BlockSpec((None, ts, D_MODEL), lambda b, j: (b, j, 0)),
                  pl.BlockSpec((2 * LRU_HEADS, D_MODEL, LRU_BLOCK), lambda b, j: (0, 0, 0),
                               pipeline_mode=pl.Buffered(1)),
                  pl.BlockSpec((None, 1, D_RNN), lambda b, j: (b, 0, 0)),
                  pl.BlockSpec((None, CONV_WIDTH - 1, D_RNN), lambda b, j: (b, 0, 0)),
                  pl.BlockSpec((CONV_WIDTH, D_RNN), row2),
                  pl.BlockSpec((1, D_RNN), row2),
                  pl.BlockSpec((LRU_HEADS, LRU_BLOCK, LRU_BLOCK), lambda b, j: (0, 0, 0)),
                  pl.BlockSpec((1, D_RNN), row2),
                  pl.BlockSpec((LRU_HEADS, LRU_BLOCK, LRU_BLOCK), lambda b, j: (0, 0, 0)),
                  pl.BlockSpec((1, D_RNN), row2),
                  pl.BlockSpec((1, D_RNN), row2)],
        out_specs=[pl.BlockSpec((None, ts, D_RNN), lambda b, j: (b, j, 0)),
                   pl.BlockSpec((None, 1, D_RNN), lambda b, j: (b, 0, 0)),
                   pl.BlockSpec((None, CONV_WIDTH - 1, D_RNN), lambda b, j: (b, 0, 0))],
        out_shape=[jax.ShapeDtypeStruct((bsz, s, D_RNN), BF16),
                   jax.ShapeDtypeStruct((bsz, 1, D_RNN), F32),
                   jax.ShapeDtypeStruct((bsz, CONV_WIDTH - 1, D_RNN), F32)],
        scratch_shapes=[pltpu.VMEM((SUBLANES + ts, D_RNN), F32), tile_f32, tile_f32, tile_f32, tile_f32,
                        pltpu.VMEM((SUBLANES, D_RNN), F32)],
        compiler_params=_cparams("arbitrary", "arbitrary"),
        name="lru_mix",
    )(x, w_in, h0, conv0, cw, cb, wa, ba, wx, bx, lam)


def _emit_routed(x1, route, outs, dense_out, rows=slice(None)):
    if dense_out:
        x1e_ref, x1b_ref, gates_ref = outs
        x1b_ref[rows, :] = x1.astype(BF16)
        gates_ref[rows, :] = route
    else:
        x1e_ref, meta_ref = outs
        meta_ref[:, rows] = route.T[0:SUBLANES, :]
    x1e_ref[rows, :D_MODEL] = x1
    x1e_ref[rows, D_MODEL:] = route


def _out_kernel(y_ref, w_ref, x_ref, g_ref, b_ref, wr_a_ref, wr_b_ref, rb_ref, *outs, dense_out, tm):
    *outs, cnt_ref = outs

    @pl.when(pl.program_id(0) == 0)
    def _():
        cnt_ref[...] = jnp.zeros_like(cnt_ref)

    for r0 in range(0, tm, ROUTE_ROWS):
        rows = slice(r0, r0 + ROUTE_ROWS)
        mix = jnp.dot(y_ref[rows, :], w_ref[...], preferred_element_type=F32)
        x1 = _layer_norm(DEEPNORM_ALPHA * x_ref[rows, :] + mix, g_ref[...], b_ref[...])
        route = _route(x1, wr_a_ref, wr_b_ref, rb_ref, cnt_ref)
        _emit_routed(x1, route, outs, dense_out, rows)


def _routed_out_specs(lead, rows, tile, dense_out):
    blk = (None,) * (len(lead) - 1)
    n_rows = lead[-1]
    specs = [pl.BlockSpec(blk + (rows, ROW_WIDTH), tile)]
    shapes = [jax.ShapeDtypeStruct(lead + (ROW_WIDTH,), F32)]
    if dense_out:
        specs += [pl.BlockSpec(blk + (rows, D_MODEL), tile), pl.BlockSpec(blk + (rows, ROUTER_LANES), tile)]
        shapes += [jax.ShapeDtypeStruct(lead + (D_MODEL,), BF16), jax.ShapeDtypeStruct(lead + (ROUTER_LANES,), F32)]
    else:
        specs += [pl.BlockSpec(blk + (None, SUBLANES, rows), lambda *i: tile(*i) + (0,))]
        shapes += [jax.ShapeDtypeStruct(lead[:-1] + (n_rows // rows, SUBLANES, rows), F32)]
    return specs, shapes


def _out_proj(y, w, x, g, b, wr_a, wr_b, rb, tm, dense_out):
    t = x.shape[0]
    full = lambda i: (0, 0)
    tile = lambda i: (i, 0)
    specs, shapes = _routed_out_specs((t,), tm, tile, dense_out)
    return pl.pallas_call(
        functools.partial(_out_kernel, dense_out=dense_out, tm=tm),
        grid=(t // tm,),
        in_specs=[pl.BlockSpec((tm, D_RNN), tile),
                  pl.BlockSpec((D_RNN, D_MODEL), full),
                  pl.BlockSpec((tm, D_MODEL), tile),
                  pl.BlockSpec((1, D_MODEL), full),
                  pl.BlockSpec((1, D_MODEL), full),
                  pl.BlockSpec((D_MODEL, 2 * ROUTER_LANES), full),
                  pl.BlockSpec((D_MODEL, ROUTER_LANES), full),
                  pl.BlockSpec((1, ROUTER_LANES), full)],
        out_specs=specs + [pl.BlockSpec((SUBLANES, ROUTER_LANES), full)],
        out_shape=shapes + [jax.ShapeDtypeStruct((SUBLANES, ROUTER_LANES), F32)],
        compiler_params=_cparams("arbitrary"),
        name="lru_out_ln_route",
    )(y, w, x, g, b, wr_a, wr_b, rb)


def _pool_kernel(x_ref, p0_ref, pw_ref, sc_ref, g_ref, b_ref, wr_a_ref, wr_b_ref, rb_ref,
                 *outs, ts, pos0, dense_out):
    *outs, pn_ref, cnt_ref, ext, lvl_a, lvl_b = outs
    j = pl.program_id(1)
    pad, hist = SUBLANES, POOL_STATE + 1
    top = pad + hist + ts

    @pl.when((pl.program_id(0) == 0) & (j == 0))
    def _():
        cnt_ref[...] = jnp.zeros_like(cnt_ref)
        ext[0:pad, :] = jnp.zeros((pad, D_MODEL), F32)
        lvl_a[0:pad, :] = jnp.zeros((pad, POOL_GROUP_DIM), F32)
        lvl_b[0:pad, :] = jnp.zeros((pad, POOL_GROUP_DIM), F32)

    @pl.when(j == 0)
    def _():
        ext[pad:pad + hist, :] = p0_ref[...]

    x = x_ref[...]
    ext[pad + hist:top, :] = x
    pos = (pos0 + j * ts + lax.broadcasted_iota(jnp.int32, (ts, 1), 0)).astype(F32)
    mixes = []
    for g, w in enumerate(POOL_WINDOWS):
        lo, hi = g * POOL_GROUP_DIM, (g + 1) * POOL_GROUP_DIM
        cur, cols = ext, slice(lo, hi)
        for lvl in range(g + 1):
            shift = 1 << lvl
            nxt = cur[pad:top, cols] + cur[pad - shift:top - shift, cols]
            if lvl < g:
                buf = lvl_a if lvl % 2 == 0 else lvl_b
                buf[pad:top, :] = nxt
                cur, cols = buf, slice(None)
        mean = nxt[hist:, :] / jnp.minimum(jnp.float32(w), pos + 1.0)
        diff = (mean - x[:, lo:hi]).astype(BF16)
        mixes.append(jnp.dot(diff, pw_ref[g], preferred_element_type=F32))
    mix = jnp.concatenate(mixes, axis=1) * sc_ref[...]
    pn_ref[...] = ext[pad + ts:top, :]
    ext[pad:pad + hist, :] = ext[pad + ts:top, :]
    x1 = _layer_norm(DEEPNORM_ALPHA * x + mix, g_ref[...], b_ref[...])
    for r0 in range(0, ts, ROUTE_ROWS):
        rows = slice(r0, min(r0 + ROUTE_ROWS, ts))
        route = _route(x1[rows, :], wr_a_ref, wr_b_ref, rb_ref, cnt_ref)
        _emit_routed(x1[rows, :], route, outs, dense_out, rows)


def _pool_mix(x, bsz, s, row0, p0, pw, sc, g, b, wr_a, wr_b, rb, ts, pos0, dense_out):
    hist = POOL_STATE + 1
    full = lambda bb, j: (0, 0)
    tile = lambda bb, j: (bb, j, 0)
    specs, shapes = _routed_out_specs((bsz, s), ts, tile, dense_out)
    assert row0 % ts == 0 and s % ts == 0
    return pl.pallas_call(
        functools.partial(_pool_kernel, ts=ts, pos0=pos0, dense_out=dense_out),
        grid=(bsz, s // ts),
        in_specs=[pl.BlockSpec((ts, D_MODEL), lambda bb, j: (row0 // ts + bb * (s // ts) + j, 0)),
                  pl.BlockSpec((None, hist, D_MODEL), lambda bb, j: (bb, 0, 0)),
                  pl.BlockSpec((POOL_GROUPS, POOL_GROUP_DIM, POOL_GROUP_DIM), lambda bb, j: (0, 0, 0)),
                  pl.BlockSpec((1, D_MODEL), full),
                  pl.BlockSpec((1, D_MODEL), full),
                  pl.BlockSpec((1, D_MODEL), full),
                  pl.BlockSpec((D_MODEL, 2 * ROUTER_LANES), full),
                  pl.BlockSpec((D_MODEL, ROUTER_LANES), full),
                  pl.BlockSpec((1, ROUTER_LANES), full)],
        out_specs=specs + [pl.BlockSpec((None, hist, D_MODEL), lambda bb, j: (bb, 0, 0)),
                           pl.BlockSpec((SUBLANES, ROUTER_LANES), full)],
        out_shape=shapes + [jax.ShapeDtypeStruct((bsz, hist, D_MODEL), F32),
                            jax.ShapeDtypeStruct((SUBLANES, ROUTER_LANES), F32)],
        scratch_shapes=[pltpu.VMEM((SUBLANES + hist + ts, D_MODEL), F32),
                        pltpu.VMEM((SUBLANES + hist + ts, POOL_GROUP_DIM), F32),
                        pltpu.VMEM((SUBLANES + hist + ts, POOL_GROUP_DIM), F32)],
        compiler_params=_cparams("arbitrary", "arbitrary"),
        name="pool_mix_ln_route",
    )(x, p0, pw, sc, g, b, wr_a, wr_b, rb)


def _moe_kernel(xb_ref, gates_ref, wgu_ref, wd_ref, x1_ref, g_ref, b_ref, o_ref, acc):
    e = pl.program_id(1)

    @pl.when(e == 0)
    def _():
        acc[...] = jnp.zeros_like(acc)

    hu = jnp.dot(xb_ref[...], wgu_ref[...], preferred_element_type=F32)
    hg, hv = hu[:, :D_EXPERT], hu[:, D_EXPERT:]
    gates = gates_ref[...]
    lane = lax.broadcasted_iota(jnp.int32, gates.shape, 1)
    gate = jnp.sum(jnp.where(lane == e + EXPERT_LANE0, gates, 0.0), axis=1, keepdims=True)
    act = (jax.nn.silu(hg) * hv * gate).astype(BF16)
    acc[...] += jnp.dot(act, wd_ref[...], preferred_element_type=F32)

    @pl.when(e == N_EXPERTS - 1)
    def _():
        o_ref[...] = _layer_norm(DEEPNORM_ALPHA * x1_ref[...] + acc[...], g_ref[...], b_ref[...])


def _moe(xb, gates, wgu, wd, layer, x1, g, b, tm):
    t = xb.shape[0]
    tile = lambda i, e: (i, 0)
    full = lambda i, e: (0, 0)
    return pl.pallas_call(
        _moe_kernel,
        grid=(t // tm, N_EXPERTS),
        in_specs=[pl.BlockSpec((tm, D_MODEL), tile),
                  pl.BlockSpec((tm, ROUTER_LANES), tile),
                  pl.BlockSpec((None, None, D_MODEL, 2 * D_EXPERT), lambda i, e: (layer, e, 0, 0)),
                  pl.BlockSpec((None, None, D_EXPERT, D_MODEL), lambda i, e: (layer, e, 0, 0)),
                  pl.BlockSpec((tm, D_MODEL), tile),
                  pl.BlockSpec((1, D_MODEL), full),
                  pl.BlockSpec((1, D_MODEL), full)],
        out_specs=pl.BlockSpec((tm, D_MODEL), tile),
        out_shape=jax.ShapeDtypeStruct((t, D_MODEL), F32),
        scratch_shapes=[pltpu.VMEM((tm, D_MODEL), F32)],
        compiler_params=_cparams("arbitrary", "arbitrary"),
        name="moe_ffn_ln",
    )(xb, gates, wgu, wd, x1, g, b)


def _dispatch(meta, counts, tm):
    t = meta.shape[0] * meta.shape[-1]
    n_tiles = t // tm
    n_steps_max = n_tiles + N_CLASSES - 1
    cpos = jnp.arange(N_CLASSES, dtype=jnp.int32)
    slots = jnp.array(PAIR_SLOTS, jnp.int32)
    exp_ab = MOE_EXPERTS_PER_GROUP * (cpos // PAIRS_PER_GROUP)[:, None] + slots[cpos % PAIRS_PER_GROUP]
    pos = meta[:, CLASS_LANE, :].reshape(t).astype(jnp.int32)
    rank = meta[:, RANK_LANE, :].reshape(t).astype(jnp.int32)
    counts = counts[0, :N_CLASSES].astype(jnp.int32)
    cend = jnp.cumsum(counts)
    cstart = cend - counts
    dest = cstart[pos] + rank
    src = jnp.zeros((t,), jnp.int32).at[dest].set(jnp.arange(t, dtype=jnp.int32))
    span = jnp.where(counts > 0, (cend - 1) // tm - cstart // tm + 1, 0)
    send = jnp.cumsum(span)
    sstart = send - span
    n_steps = send[-1]
    step = jnp.arange(n_steps_max, dtype=jnp.int32)
    live = step < n_steps
    cls = jnp.sum((step[:, None] >= send[None, :]).astype(jnp.int32), axis=1)
    cls = jnp.where(live, cls, cls[n_steps - 1])
    tile = jnp.where(live, cstart[cls] // tm + step - sstart[cls], n_tiles - 1)
    prev_tile = jnp.concatenate([jnp.full((1,), -1, jnp.int32), tile[:-1]])
    next_tile = jnp.concatenate([tile[1:], jnp.full((1,), -1, jnp.int32)])
    first = live & (tile != prev_tile)
    last = live & ((tile != next_tile) | (step == n_steps - 1))
    i32 = lambda v: v.astype(jnp.int32)
    return (src.reshape(n_tiles, 1, tm), i32(tile), exp_ab[cls, 0], exp_ab[cls, 1], i32(cls),
            i32(first), i32(last), i32(n_steps).reshape(1))


def _moe_sparse_kernel(tile_ref, ea_ref, eb_ref, cls_ref, first_ref, last_ref, nst_ref,
                       src_ref, srcn_ref, x1_hbm, wgu_a, wgu_b, wd_a, wd_b, g_ref, b_ref,
                       x2_hbm, xbuf, obuf, acc, gsem, ssem, *, tm, n_tiles):
    del ea_ref, eb_ref
    s = pl.program_id(0)
    live = s < nst_ref[0]
    tau = tile_ref[s]
    slot = tau % 2
    first = first_ref[s] == 1
    last = last_ref[s] == 1

    def start_gather(idx_ref, sl):
        for r in range(tm):
            pltpu.make_async_copy(x1_hbm.at[pl.ds(idx_ref[0, r], 1)], xbuf.at[sl, pl.ds(r, 1)], gsem.at[sl]).start()

    def wait_gather(sl):
        pltpu.make_async_copy(x1_hbm.at[pl.ds(0, tm)], xbuf.at[sl], gsem.at[sl]).wait()

    def wait_scatter(sl):
        pltpu.make_async_copy(obuf.at[sl], x2_hbm.at[pl.ds(0, tm)], ssem.at[sl]).wait()

    def start_scatter(sl):
        for r in range(tm):
            pltpu.make_async_copy(obuf.at[sl, pl.ds(r, 1)], x2_hbm.at[pl.ds(src_ref[0, r], 1)], ssem.at[sl]).start()

    def for_slot(value, fn):
        for sl in (0, 1):
            pl.when(value == sl)(functools.partial(fn, sl))

    @pl.when(s == 0)
    def _():
        acc[...] = jnp.zeros_like(acc)
        start_gather(src_ref, 0)

    @pl.when(live & first)
    def _():
        wait_gather(slot)

        @pl.when(tau + 1 < n_tiles)
        def _():
            for_slot(1 - slot, functools.partial(start_gather, srcn_ref))

    @pl.when(live)
    def _():
        xb = xbuf[slot, :, :D_MODEL].astype(BF16)
        route = xbuf[slot, :, D_MODEL:]
        mine = route[:, CLASS_LANE:CLASS_LANE + 1] == cls_ref[s].astype(F32)
        out = None
        for col, (wgu, wd) in enumerate(((wgu_a, wd_a), (wgu_b, wd_b))):
            gate = jnp.where(mine, route[:, GATE_LANE0 + col:GATE_LANE0 + col + 1], 0.0)
            hu = jnp.dot(xb, wgu[...], preferred_element_type=F32)
            act = (jax.nn.silu(hu[:, :D_EXPERT]) * hu[:, D_EXPERT:] * gate).astype(BF16)
            o = jnp.dot(act, wd[...], preferred_element_type=F32)
            out = o if out is None else out + o

        acc[...] = jnp.where(first, out, acc[...] + out)

    @pl.when(live & last)
    def _():
        @pl.when(tau >= 2)
        def _():
            wait_scatter(slot)

        obuf[slot] = _layer_norm(DEEPNORM_ALPHA * xbuf[slot, :, :D_MODEL] + acc[...], g_ref[...], b_ref[...])
        for_slot(slot, start_scatter)

    @pl.when(s == nst_ref[0] - 1)
    def _():
        wait_scatter(slot)

        @pl.when(tau >= 1)
        def _():
            wait_scatter(1 - slot)


def _moe_sparse(x1e, meta, counts, wgu, wd, layer, g, b, tm):
    t = x1e.shape[0]
    src, tile, exp_a, exp_b, cls, first, last, n_steps = _dispatch(meta, counts, tm)
    n_tiles = t // tm
    full = lambda s, *_: (0, 0)
    grid_spec = pltpu.PrefetchScalarGridSpec(
        num_scalar_prefetch=7,
        grid=(tile.shape[0],),
        in_specs=[pl.BlockSpec((None, 1, tm), lambda s, tl, *_: (tl[s], 0, 0), memory_space=pltpu.SMEM),
                  pl.BlockSpec((None, 1, tm), lambda s, tl, *_: (jnp.minimum(tl[s] + 1, n_tiles - 1), 0, 0),
                               memory_space=pltpu.SMEM),
                  pl.BlockSpec(memory_space=pl.ANY),
                  pl.BlockSpec((None, None, D_MODEL, 2 * D_EXPERT), lambda s, tl, ea, eb, *_: (layer, ea[s], 0, 0)),
                  pl.BlockSpec((None, None, D_MODEL, 2 * D_EXPERT), lambda s, tl, ea, eb, *_: (layer, eb[s], 0, 0)),
                  pl.BlockSpec((None, None, D_EXPERT, D_MODEL), lambda s, tl, ea, eb, *_: (layer, ea[s], 0, 0)),
                  pl.BlockSpec((None, None, D_EXPERT, D_MODEL), lambda s, tl, ea, eb, *_: (layer, eb[s], 0, 0)),
                  pl.BlockSpec((1, D_MODEL), full),
                  pl.BlockSpec((1, D_MODEL), full)],
        out_specs=pl.BlockSpec(memory_space=pl.ANY),
        scratch_shapes=[pltpu.VMEM((2, tm, ROW_WIDTH), F32), pltpu.VMEM((2, tm, D_MODEL), F32),
                        pltpu.VMEM((tm, D_MODEL), F32),
                        pltpu.SemaphoreType.DMA((2,)), pltpu.SemaphoreType.DMA((2,))],
    )
    return pl.pallas_call(
        functools.partial(_moe_sparse_kernel, tm=tm, n_tiles=n_tiles),
        grid_spec=grid_spec,
        out_shape=jax.ShapeDtypeStruct((t, D_MODEL), F32),
        compiler_params=_cparams("arbitrary"),
        name="moe_sparse_ln",
    )(tile, exp_a, exp_b, cls, first, last, n_steps, src, src, x1e, wgu, wgu, wd, wd, g, b)


def _router_weights(wg, bg, we, be):
    w = jnp.concatenate([wg, we], axis=1)
    w = jnp.pad(w, ((0, 0), (0, ROUTER_LANES - w.shape[1])))
    w_hi = w.astype(BF16)
    w_lo = (w - w_hi.astype(F32)).astype(BF16)
    rb = jnp.pad(jnp.concatenate([bg, be]), (0, ROUTER_LANES - MOE_GROUPS - N_EXPERTS))
    return jnp.concatenate([w_hi, w_lo], axis=1), w_hi, rb.reshape(1, ROUTER_LANES)


def _trunk(x, lru_h0, lru_conv0, pool0, pos0, wts, ts, tm, sparse):
    bsz, s, d = x.shape
    t = bsz * s
    row = lambda v: v.reshape(1, -1)

    def moe(layer, routed, counts):
        g, b = row(wts["ln_g"][layer, 1]), row(wts["ln_b"][layer, 1])
        if sparse:
            x1e, meta = routed
            meta = meta.reshape(-1, SUBLANES, meta.shape[-1])
            return _moe_sparse(x1e.reshape(t, ROW_WIDTH), meta, counts, wts["wgu"], wts["wd"], layer, g, b, MOE_TILE)
        x1e, x1b, gates = (v.reshape(t, -1) for v in routed)
        return _moe(x1b, gates, wts["wgu"], wts["wd"], layer, x1e[:, :D_MODEL], g, b, min(512, t))

    y, h_last, conv_new = _lru_mix(
        x, wts["w_in"], lru_h0.reshape(bsz, 1, D_RNN), lru_conv0, 0.5 * wts["conv_w"], row(0.5 * wts["conv_b"]),
        wts["w_a"], row(0.5 * wts["b_a"]), wts["w_x"], row(0.5 * wts["b_x"]), row(wts["lam"]), ts)
    *routed, counts = _out_proj(y.reshape(t, D_RNN), wts["w_out"], x.reshape(t, d),
                                row(wts["ln_g"][0, 0]), row(wts["ln_b"][0, 0]), *wts["router"][0], tm, not sparse)
    x2 = moe(0, routed, counts)
    p0 = jnp.pad(pool0, ((0, 0), (1, 0), (0, 0)))
    *routed, pool_new, counts = _pool_mix(
        x2, bsz, s, 0, p0, wts["pool_w"], row(wts["pool_scale"]),
        row(wts["ln_g"][1, 0]), row(wts["ln_b"][1, 0]), *wts["router"][1], ts, pos0, not sparse)
    y_out = moe(1, routed, counts)
    return (y_out.reshape(bsz, s, d), h_last.reshape(1, bsz, D_RNN), conv_new[None],
            pool_new[None, :, 1:, :])


def kernel(x_prompt, x_sample, state_lru_h, state_lru_conv, state_pool, lru_w_in, lru_conv_w, lru_conv_b, lru_w_a, lru_b_a, lru_w_x, lru_b_x, lru_lambda, lru_w_out, pool_w, pool_scale, ln_g, ln_b, moe_router_g_w, moe_router_g_b, moe_router_e_w, moe_router_e_b, moe_w_gate_up, moe_w_down):
    wts = {
        "w_in": lru_w_in[0].astype(BF16).reshape(D_MODEL, 2 * LRU_HEADS, LRU_BLOCK).transpose(1, 0, 2), "conv_w": lru_conv_w[0], "conv_b": lru_conv_b[0],
        "w_a": lru_w_a[0].astype(BF16), "b_a": lru_b_a[0], "w_x": lru_w_x[0].astype(BF16), "b_x": lru_b_x[0],
        "lam": lru_lambda[0], "w_out": lru_w_out[0].astype(BF16),
        "pool_w": pool_w[0].astype(BF16), "pool_scale": pool_scale[0], "ln_g": ln_g, "ln_b": ln_b,
        "router": [_router_weights(moe_router_g_w[i], moe_router_g_b[i], moe_router_e_w[i], moe_router_e_b[i])
                   for i in range(DEPTH)],
        "wgu": moe_w_gate_up.astype(BF16), "wd": moe_w_down.astype(BF16),
    }
    bp = x_prompt.shape[0]
    dt = x_prompt.dtype
    y_p, h_p, c_p, p_p = _trunk(
        x_prompt, jnp.zeros((bp, D_RNN), dt), jnp.zeros((bp, CONV_WIDTH - 1, D_RNN), dt),
        jnp.zeros((bp, POOL_STATE, D_MODEL), dt), 0, wts, ts=256, tm=512, sparse=True)
    y_s, h_s, c_s, p_s = _trunk(
        x_sample, state_lru_h[0], state_lru_conv[0], state_pool[0], PAST_LEN, wts,
        ts=x_sample.shape[1], tm=512, sparse=False)
    return (y_p, y_s, h_p, c_p, p_p, h_s, c_s, p_s)
```
